```python
import jax
import jax.numpy as jnp
from jax import lax
import numpy as np

D_MODEL = 2048
BATCH = 2
SEQ = 4096
DEPTH = 4
DEC_BATCH = 128
DEC_SEQ = 4
PAST_LEN = 8192
PAGE_SIZE = 128

N_MIXERS = 3
Q_BLOCK = 128
ROPE_THETA = 10000.0
NORM_EPS = 1e-6
NEG_INF = -1e30

SB_HEADS = 16
SB_KV_HEADS = 4
SB_HEAD_DIM = D_MODEL // SB_HEADS
SB_GROUPS = SB_HEADS // SB_KV_HEADS
SB_WIDTH = SB_HEADS * SB_HEAD_DIM
SB_KV_WIDTH = SB_KV_HEADS * SB_HEAD_DIM
SB_IN = SB_WIDTH + 2 * SB_KV_WIDTH + SB_WIDTH

SWA_HEADS = 32
SWA_KV_HEADS = 8
SWA_HEAD_DIM = D_MODEL // SWA_HEADS
SWA_GROUPS = SWA_HEADS // SWA_KV_HEADS
SWA_WIDTH = SWA_HEADS * SWA_HEAD_DIM
SWA_KV_WIDTH = SWA_KV_HEADS * SWA_HEAD_DIM
SWA_IN = SWA_WIDTH + 2 * SWA_KV_WIDTH + SWA_WIDTH
WINDOW = 128

MLA_HEADS = 16
MLA_Q_LORA = D_MODEL // 4
MLA_KV_LORA = D_MODEL // 8
MLA_NOPE = 128
MLA_ROPE = 64
MLA_QK = MLA_NOPE + MLA_ROPE
MLA_V = 128
MLA_WIDTH = MLA_HEADS * MLA_V
MLA_IN = MLA_Q_LORA + MLA_KV_LORA + MLA_ROPE + MLA_WIDTH

N_SB_LAYERS = len(range(0, DEPTH, N_MIXERS))
N_SWA_LAYERS = len(range(1, DEPTH, N_MIXERS))
N_MLA_LAYERS = len(range(2, DEPTH, N_MIXERS))

kernel_name = "hybrid_stickbreak_swasink_mla_step"


def _rmsnorm(x, g):
    xf = x.astype(jnp.float32)
    y = xf * lax.rsqrt(jnp.mean(xf * xf, axis=-1, keepdims=True) + NORM_EPS)
    return (y * g.astype(jnp.float32)).astype(x.dtype)


def _rope(x, pos):
    d = x.shape[-1]
    half = d // 2
    inv = jnp.float32(ROPE_THETA) ** (-jnp.arange(half, dtype=jnp.float32) * 2.0 / d)
    ang = pos.astype(jnp.float32)[:, None] * inv[None, :]
    cos = jnp.cos(ang)[:, None, :]
    sin = jnp.sin(ang)[:, None, :]
    xf = x.astype(jnp.float32)
    x1, x2 = xf[..., :half], xf[..., half:]
    return jnp.concatenate([x1 * cos - x2 * sin, x1 * sin + x2 * cos], axis=-1).astype(x.dtype)


def _to_blocks(a):
    b, s = a.shape[0], a.shape[1]
    return jnp.swapaxes(a.reshape((b, s // Q_BLOCK, Q_BLOCK) + a.shape[2:]), 0, 1)


def _from_blocks(a):
    nb, b, blk = a.shape[0], a.shape[1], a.shape[2]
    return jnp.swapaxes(a, 0, 1).reshape((b, nb * blk) + a.shape[3:])


def _gather_pages(cache, j, page_table):
    rows = cache[j, page_table]
    return rows.reshape((page_table.shape[0], page_table.shape[1] * cache.shape[2]) + cache.shape[3:])


def _sb_attend(q, k, v, q_pos, k_pos):
    z = jnp.einsum('bqhgd,bkhd->bhgqk', q, k).astype(jnp.float32) * (SB_HEAD_DIM ** -0.5)
    causal = k_pos[None, :] < q_pos[:, None]
    log_keep = jnp.where(causal, jax.nn.log_sigmoid(-z), 0.0)
    log_surv = lax.cumsum(log_keep, axis=4, reverse=True) - log_keep
    w = jnp.where(causal, jnp.exp(jax.nn.log_sigmoid(z) + log_surv), 0.0)
    return jnp.einsum('bhgqk,bkhd->bqhgd', w.astype(v.dtype), v)


def _sb_layer(hp, hs, cache_k, cache_v, j, page_table, w_in, w_out):
    def project(h):
        b, t = h.shape[0], h.shape[1]
        p = h @ w_in
        q = p[..., :SB_WIDTH].reshape(b, t, SB_KV_HEADS, SB_GROUPS, SB_HEAD_DIM)
        k = p[..., SB_WIDTH:SB_WIDTH + SB_KV_WIDTH].reshape(b, t, SB_KV_HEADS, SB_HEAD_DIM)
        v = p[..., SB_WIDTH + SB_KV_WIDTH:SB_WIDTH + 2 * SB_KV_WIDTH].reshape(b, t, SB_KV_HEADS, SB_HEAD_DIM)
        gate = p[..., SB_WIDTH + 2 * SB_KV_WIDTH:]
        return q, k, v, gate

    def finish(o, gate):
        b, t = o.shape[0], o.shape[1]
        return (o.reshape(b, t, SB_WIDTH) * jax.nn.silu(gate)) @ w_out

    qp, kp, vp, gp = project(hp)
    pos_p = jnp.arange(hp.shape[1])
    op = lax.map(lambda blk: _sb_attend(blk[0], kp, vp, blk[1], pos_p),
                 (_to_blocks(qp), pos_p.reshape(-1, Q_BLOCK)))
    yp = finish(_from_blocks(op), gp)

    qs, ks, vs, gs = project(hs)
    k_all = jnp.concatenate([_gather_pages(cache_k, j, page_table), ks], axis=1)
    v_all = jnp.concatenate([_gather_pages(cache_v, j, page_table), vs], axis=1)
    pos_all = jnp.arange(PAST_LEN + hs.shape[1])
    os_ = _sb_attend(qs, k_all, v_all, pos_all[PAST_LEN:], pos_all)
    ys = finish(os_, gs)
    return yp, ys, (kp, vp, ks, vs)


def _swa_attend(q, k, v, q_pos, k_pos, sinks):
    s = jnp.einsum('...qhgd,...khd->...hgqk', q, k).astype(jnp.float32) * (SWA_HEAD_DIM ** -0.5)
    diff = q_pos[..., :, None] - k_pos[..., None, :]
    valid = (diff >= 0) & (diff < WINDOW) & (k_pos[..., None, :] >= 0)
    s = jnp.where(valid[..., None, None, :, :], s, NEG_INF)
    sink = sinks.astype(jnp.float32).reshape(SWA_KV_HEADS, SWA_GROUPS, 1, 1)
    m = jnp.maximum(jnp.max(s, axis=-1, keepdims=True), sink)
    p = jnp.exp(s - m)
    p = p / (jnp.sum(p, axis=-1, keepdims=True) + jnp.exp(sink - m))
    return jnp.einsum('...hgqk,...khd->...qhgd', p.astype(v.dtype), v)


def _swa_layer(hp, hs, state_k, state_v, j, w_in, sinks, w_out):
    def project(h, pos):
        b, t = h.shape[0], h.shape[1]
        p = h @ w_in
        q = _rope(p[..., :SWA_WIDTH].reshape(b, t, SWA_HEADS, SWA_HEAD_DIM), pos)
        q = q.reshape(b, t, SWA_KV_HEADS, SWA_GROUPS, SWA_HEAD_DIM)
        k = _rope(p[..., SWA_WIDTH:SWA_WIDTH + SWA_KV_WIDTH].reshape(b, t, SWA_KV_HEADS, SWA_HEAD_DIM), pos)
        v = p[..., SWA_WIDTH + SWA_KV_WIDTH:SWA_WIDTH + 2 * SWA_KV_WIDTH].reshape(b, t, SWA_KV_HEADS, SWA_HEAD_DIM)
        gate = p[..., SWA_WIDTH + 2 * SWA_KV_WIDTH:]
        return q, k, v, gate

    def finish(o, gate):
        b, t = gate.shape[0], gate.shape[1]
        return (o.reshape(b, t, SWA_WIDTH) * jax.nn.silu(gate)) @ w_out

    b, s = hp.shape[0], hp.shape[1]
    nb = s // WINDOW
    pos_p = jnp.arange(s)
    qp, kp, vp, gp = project(hp, pos_p)
    q_b = qp.reshape(b, nb, WINDOW, SWA_KV_HEADS, SWA_GROUPS, SWA_HEAD_DIM)
    pad = ((0, 0), (WINDOW, 0), (0, 0), (0, 0))

    def band(a):
        prev = jnp.pad(a, pad)[:, :s].reshape(b, nb, WINDOW, SWA_KV_HEADS, SWA_HEAD_DIM)
        return jnp.concatenate([prev, a.reshape(b, nb, WINDOW, SWA_KV_HEADS, SWA_HEAD_DIM)], axis=2)

    qpos_b = pos_p.reshape(nb, WINDOW)
    kpos_b = jnp.concatenate([qpos_b - WINDOW, qpos_b], axis=1)
    op = _swa_attend(q_b, band(kp), band(vp), qpos_b, kpos_b, sinks)
    yp = finish(op, gp)
    keep_p = min(WINDOW, s)

    t = hs.shape[1]
    w_buf = state_k.shape[2]
    qpos_s = PAST_LEN + jnp.arange(t)
    qs, ks, vs, gs = project(hs, qpos_s)
    k_all = jnp.concatenate([state_k[j], ks], axis=1)
    v_all = jnp.concatenate([state_v[j], vs], axis=1)
    kpos_s = PAST_LEN - w_buf + jnp.arange(w_buf + t)
    os_ = _swa_attend(qs, k_all, v_all, qpos_s, kpos_s, sinks)
    ys = finish(os_, gs)
    return yp, ys, (kp[:, s - keep_p:], vp[:, s - keep_p:], k_all[:, t:], v_all[:, t:])


def _mla_attend(q_lat, q_pe, c_kv, k_pe, q_pos, k_pos):
    s = (jnp.einsum('bqhc,bkc->bhqk', q_lat, c_kv) + jnp.einsum('bqhr,bkr->bhqk', q_pe, k_pe))
    s = s.astype(jnp.float32) * (MLA_QK ** -0.5)
    s = jnp.where(k_pos[None, :] <= q_pos[:, None], s, NEG_INF)
    p = jax.nn.softmax(s, axis=-1)
    return jnp.einsum('bhqk,bkc->bqhc', p.astype(c_kv.dtype), c_kv)


def _mla_layer(hp, hs, cache_ckv, cache_kpe, j, page_table, w_in, q_norm, w_uq, kv_norm, w_uk, w_uv, w_out):
    def project(h, pos):
        b, t = h.shape[0], h.shape[1]
        p = h @ w_in
        c_q = _rmsnorm(p[..., :MLA_Q_LORA], q_norm)
        c_kv = _rmsnorm(p[..., MLA_Q_LORA:MLA_Q_LORA + MLA_KV_LORA], kv_norm)
        o = MLA_Q_LORA + MLA_KV_LORA
        k_pe = _rope(p[..., o:o + MLA_ROPE][:, :, None, :], pos)[:, :, 0, :]
        gate = p[..., o + MLA_ROPE:]
        q = (c_q @ w_uq).reshape(b, t, MLA_HEADS, MLA_QK)
        q_pe = _rope(q[..., MLA_NOPE:], pos)
        q_lat = jnp.einsum('bthn,chn->bthc', q[..., :MLA_NOPE], w_uk)
        return q_lat, q_pe, c_kv, k_pe, gate

    def finish(o_lat, gate):
        b, t = gate.shape[0], gate.shape[1]
        o = jnp.einsum('bthc,chv->bthv', o_lat, w_uv).reshape(b, t, MLA_WIDTH)
        return (o * jax.nn.silu(gate)) @ w_out

    pos_p = jnp.arange(hp.shape[1])
    qlp, qpp, ckvp, kpep, gp = project(hp, pos_p)
    op = lax.map(lambda blk: _mla_attend(blk[0], blk[1], ckvp, kpep, blk[2], pos_p),
                 (_to_blocks(qlp), _to_blocks(qpp), pos_p.reshape(-1, Q_BLOCK)))
    yp = finish(_from_blocks(op), gp)

    t = hs.shape[1]
    qpos_s = PAST_LEN + jnp.arange(t)
    qls, qps, ckvs, kpes, gs = project(hs, qpos_s)
    ckv_all = jnp.concatenate([_gather_pages(cache_ckv, j, page_table), ckvs], axis=1)
    kpe_all = jnp.concatenate([_gather_pages(cache_kpe, j, page_table), kpes], axis=1)
    os_ = _mla_attend(qls, qps, ckv_all, kpe_all, qpos_s, jnp.arange(PAST_LEN + t))
    ys = finish(os_, gs)
    return yp, ys, (ckvp, kpep, ckvs, kpes)


def setup_inputs(seed: int = 0) -> dict:
    key = jax.random.key(seed)
    ks = jax.random.split(key, 26)

    def nrm(k, shape, scale):
        return jax.random.normal(k, shape, jnp.float32) * scale

    n_pages = PAST_LEN // PAGE_SIZE
    n_used = DEC_BATCH * n_pages
    n_pool = n_used + n_used // 4
    w_buf = min(WINDOW, PAST_LEN)
    page_table = jax.random.permutation(ks[2], n_pool)[:n_used].reshape(DEC_BATCH, n_pages).astype(jnp.int32)
    return {
        "x_prompt": nrm(ks[0], (BATCH, SEQ, D_MODEL), 1.0),
        "x_sample": nrm(ks[1], (DEC_BATCH, DEC_SEQ, D_MODEL), 1.0),
        "cache_sb_k": nrm(ks[3], (N_SB_LAYERS, n_pool, PAGE_SIZE, SB_KV_HEADS, SB_HEAD_DIM), 1.0),
        "cache_sb_v": nrm(ks[4], (N_SB_LAYERS, n_pool, PAGE_SIZE, SB_KV_HEADS, SB_HEAD_DIM), 1.0),
        "state_swa_k": nrm(ks[5], (N_SWA_LAYERS, DEC_BATCH, w_buf, SWA_KV_HEADS, SWA_HEAD_DIM), 1.0),
        "state_swa_v": nrm(ks[6], (N_SWA_LAYERS, DEC_BATCH, w_buf, SWA_KV_HEADS, SWA_HEAD_DIM), 1.0),
        "cache_mla_ckv": nrm(ks[7], (N_MLA_LAYERS, n_pool, PAGE_SIZE, MLA_KV_LORA), 1.0),
        "cache_mla_kpe": nrm(ks[8], (N_MLA_LAYERS, n_pool, PAGE_SIZE, MLA_ROPE), 1.0),
        "page_table": page_table,
        "norm_pre": 1.0 + nrm(ks[9], (DEPTH, D_MODEL), 0.05),
        "norm_post": 1.0 + nrm(ks[10], (DEPTH, D_MODEL), 0.05),
        "sb_w_in": nrm(ks[11], (N_SB_LAYERS, D_MODEL, SB_IN), D_MODEL ** -0.5),
        "sb_w_out": nrm(ks[12], (N_SB_LAYERS, SB_WIDTH, D_MODEL), SB_WIDTH ** -0.5),
        "swa_w_in": nrm(ks[13], (N_SWA_LAYERS, D_MODEL, SWA_IN), D_MODEL ** -0.5),
        "swa_sinks": nrm(ks[14], (N_SWA_LAYERS, SWA_HEADS), 1.0),
        "swa_w_out": nrm(ks[15], (N_SWA_LAYERS, SWA_WIDTH, D_MODEL), SWA_WIDTH ** -0.5),
        "mla_w_in": nrm(ks[16], (N_MLA_LAYERS, D_MODEL, MLA_IN), D_MODEL ** -0.5),
        "mla_q_norm": 1.0 + nrm(ks[17], (N_MLA_LAYERS, MLA_Q_LORA), 0.05),
        "mla_w_uq": nrm(ks[18], (N_MLA_LAYERS, MLA_Q_LORA, MLA_HEADS * MLA_QK), MLA_Q_LORA ** -0.5),
        "mla_kv_norm": 1.0 + nrm(ks[19], (N_MLA_LAYERS, MLA_KV_LORA), 0.05),
        "mla_w_uk": nrm(ks[20], (N_MLA_LAYERS, MLA_KV_LORA, MLA_HEADS, MLA_NOPE), MLA_KV_LORA ** -0.5),
        "mla_w_uv": nrm(ks[21], (N_MLA_LAYERS, MLA_KV_LORA, MLA_HEADS, MLA_V), MLA_KV_LORA ** -0.5),
        "mla_w_out": nrm(ks[22], (N_MLA_LAYERS, MLA_WIDTH, D_MODEL), MLA_WIDTH ** -0.5),
    }


def reference(x_prompt, x_sample, cache_sb_k, cache_sb_v, state_swa_k, state_swa_v, cache_mla_ckv, cache_mla_kpe,
              page_table, norm_pre, norm_post, sb_w_in, sb_w_out, swa_w_in, swa_sinks, swa_w_out,
              mla_w_in, mla_q_norm, mla_w_uq, mla_kv_norm, mla_w_uk, mla_w_uv, mla_w_out):
    xp, xs = x_prompt, x_sample
    sb_st = ([], [], [], [])
    swa_st = ([], [], [], [])
    mla_st = ([], [], [], [])
    for i in range(DEPTH):
        kind, j = i % N_MIXERS, i // N_MIXERS
        hp = _rmsnorm(xp, norm_pre[i])
        hs = _rmsnorm(xs, norm_pre[i])
        if kind == 0:
            yp, ys, st = _sb_layer(hp, hs, cache_sb_k, cache_sb_v, j, page_table, sb_w_in[j], sb_w_out[j])
            dst = sb_st
        elif kind == 1:
            yp, ys, st = _swa_layer(hp, hs, state_swa_k, state_swa_v, j, swa_w_in[j], swa_sinks[j], swa_w_out[j])
            dst = swa_st
        else:
            yp, ys, st = _mla_layer(hp, hs, cache_mla_ckv, cache_mla_kpe, j, page_table, mla_w_in[j],
                                    mla_q_norm[j], mla_w_uq[j], mla_kv_norm[j], mla_w_uk[j], mla_w_uv[j],
                                    mla_w_out[j])
            dst = mla_st
        for lst, a in zip(dst, st):
            lst.append(a)
        xp = xp + _rmsnorm(yp, norm_post[i])
        xs = xs + _rmsnorm(ys, norm_post[i])
    return (xp, xs,
            jnp.stack(sb_st[0]), jnp.stack(sb_st[1]), jnp.stack(sb_st[2]), jnp.stack(sb_st[3]),
            jnp.stack(swa_st[0]), jnp.stack(swa_st[1]), jnp.stack(swa_st[2]), jnp.stack(swa_st[3]),
            jnp.stack(mla_st[0]), jnp.stack(mla_st[1]), jnp.stack(mla_st[2]), jnp.stack(mla_st[3]))
```

```python
import functools

import jax
import jax.numpy as jnp
from jax import lax
from jax.experimental import pallas as pl
from jax.experimental.pallas import tpu as pltpu

F32 = jnp.float32
BF16 = jnp.bfloat16

D_MODEL = 2048
DEPTH = 4
N_MIXERS = 3
PAGE_SIZE = 128
ROPE_THETA = 10000.0
NORM_EPS = 1e-6
NEG_INF = -1e30

SB_HEADS = 16
SB_KV_HEADS = 4
SB_HEAD_DIM = D_MODEL // SB_HEADS
SB_GROUPS = SB_HEADS // SB_KV_HEADS
SB_WIDTH = SB_HEADS * SB_HEAD_DIM
SB_KV_WIDTH = SB_KV_HEADS * SB_HEAD_DIM

SWA_HEADS = 32
SWA_KV_HEADS = 8
SWA_HEAD_DIM = D_MODEL // SWA_HEADS
SWA_GROUPS = SWA_HEADS // SWA_KV_HEADS
SWA_WIDTH = SWA_HEADS * SWA_HEAD_DIM
SWA_KV_WIDTH = SWA_KV_HEADS * SWA_HEAD_DIM
WINDOW = 128

MLA_HEADS = 16
MLA_Q_LORA = D_MODEL // 4
MLA_KV_LORA = D_MODEL // 8
MLA_NOPE = 128
MLA_ROPE = 64
MLA_QK = MLA_NOPE + MLA_ROPE
MLA_V = 128
MLA_WIDTH = MLA_HEADS * MLA_V

LANES = 128
ROPE_DIM = 64
VMEM_LIMIT = 56 * 1024 * 1024

ROW_TILE = 512
PAGES_PER_STEP = 8

_NT = (((1,), (1,)), ((), ()))


def _params(*sem):
    return pltpu.CompilerParams(dimension_semantics=sem, vmem_limit_bytes=VMEM_LIMIT)


def _div(x, n):
    assert n & (n - 1) == 0
    return x >> (n.bit_length() - 1)


def _mod(x, n):
    assert n & (n - 1) == 0
    return x & (n - 1)


def _silu(x):
    return x / (1.0 + jnp.exp(-x))


def _rms(x, g):
    return x * lax.rsqrt(jnp.mean(x * x, axis=-1, keepdims=True) + NORM_EPS) * g


def _rope_lanes(x, cos, sin_signed):
    w = x.shape[1]
    reps = w // LANES
    if reps > 1:
        cos = jnp.concatenate([cos] * reps, axis=1)
        sin_signed = jnp.concatenate([sin_signed] * reps, axis=1)
    lane = lax.broadcasted_iota(jnp.int32, x.shape, 1)
    first_half = (lane & (ROPE_DIM - 1)) < (ROPE_DIM // 2)
    partner = jnp.where(first_half, pltpu.roll(x, w - ROPE_DIM // 2, 1), pltpu.roll(x, ROPE_DIM // 2, 1))
    return x * cos + partner * sin_signed


def _proj_kernel(x_ref, g_ref, w_ref, cos_ref, sin_ref, o_ref, h_ref, *, rope_lo, rope_hi):
    j = pl.program_id(1)

    @pl.when(j == 0)
    def _():
        h_ref[...] = _rms(x_ref[...], g_ref[...]).astype(BF16)

    acc = jnp.dot(h_ref[...], w_ref[...], preferred_element_type=F32)
    if rope_hi > rope_lo:
        in_rope = (j >= rope_lo) & (j < rope_hi)

        @pl.when(in_rope)
        def _():
            o_ref[...] = _rope_lanes(acc, cos_ref[...], sin_ref[...])

        @pl.when(jnp.logical_not(in_rope))
        def _():
            o_ref[...] = acc
    else:
        o_ref[...] = acc


def _proj(x, g, w, cos, sin, *, tn, rope_cols=(0, 0)):
    n, d = x.shape
    c = w.shape[1]
    tm = ROW_TILE
    kern = functools.partial(_proj_kernel, rope_lo=rope_cols[0] // tn, rope_hi=rope_cols[1] // tn)
    return pl.pallas_call(
        kern,
        grid=(n // tm, c // tn),
        in_specs=[
            pl.BlockSpec((tm, d), lambda i, j: (i, 0)),
            pl.BlockSpec((1, d), lambda i, j: (0, 0)),
            pl.BlockSpec((d, tn), lambda i, j: (0, j)),
            pl.BlockSpec((tm, LANES), lambda i, j: (i, 0)),
            pl.BlockSpec((tm, LANES), lambda i, j: (i, 0)),
        ],
        out_specs=pl.BlockSpec((tm, tn), lambda i, j: (i, j)),
        out_shape=jax.ShapeDtypeStruct((n, c), F32),
        scratch_shapes=[pltpu.VMEM((tm, d), BF16)],
        compiler_params=_params("parallel", "arbitrary"),
        name="norm_proj",
    )(x, g.reshape(1, d), w, cos, sin)


def _out_kernel(o_ref, gate_ref, w_ref, x_ref, g_ref, y_ref):
    a = (o_ref[...].astype(F32) * _silu(gate_ref[...])).astype(BF16)
    y = jnp.dot(a, w_ref[...], preferred_element_type=F32)
    y_ref[...] = x_ref[...] + _rms(y, g_ref[...])


def _out(o, p, w, x, g):
    n, d = x.shape
    tm = 256
    return pl.pallas_call(
        _out_kernel,
        grid=(n // tm,),
        in_specs=[
            pl.BlockSpec((tm, d), lambda i: (i, 0)),
            pl.BlockSpec((tm, d), lambda i: (i, 0)),
            pl.BlockSpec((d, d), lambda i: (0, 0)),
            pl.BlockSpec((tm, d), lambda i: (i, 0)),
            pl.BlockSpec((1, d), lambda i: (0, 0)),
        ],
        out_specs=pl.BlockSpec((tm, d), lambda i: (i, 0)),
        out_shape=jax.ShapeDtypeStruct((n, d), F32),
        compiler_params=_params("parallel"),
        name="gate_out_norm",
    )(o, p, w, x, g.reshape(1, d))


def _suffix_ones(tk):
    s = lax.broadcasted_iota(jnp.int32, (tk, tk), 0)
    j = lax.broadcasted_iota(jnp.int32, (tk, tk), 1)
    u = (s > j).astype(BF16)
    return jnp.concatenate([u, u], axis=0)


def _sb_block(z, carry, u2, valid):
    soft = jnp.log1p(jnp.exp(-jnp.abs(z)))
    log_keep = jnp.minimum(-z, 0.0) - soft
    log_beta = log_keep + z
    if valid is not None:
        log_keep = jnp.where(valid, log_keep, 0.0)
    hi = log_keep.astype(BF16)
    lo = (log_keep - hi.astype(F32)).astype(BF16)
    later = jnp.dot(jnp.concatenate([hi, lo], axis=1), u2, preferred_element_type=F32)
    w = jnp.exp(log_beta + later + carry)
    if valid is not None:
        w = jnp.where(valid, w, 0.0)
    return w, carry + later[:, :1] + log_keep[:, :1]


def _sb_prompt_kernel(q_ref, k_ref, v_ref, u_ref, o_ref, acc_ref, carry_ref, *, tq, tk):
    i = pl.program_id(2)
    groups = SB_GROUPS
    rows = groups * tq
    scale = SB_HEAD_DIM ** -0.5
    q = jnp.concatenate([q_ref[:, g * SB_HEAD_DIM:(g + 1) * SB_HEAD_DIM] for g in range(groups)], axis=0).astype(BF16)
    u2 = u_ref[...]
    diag = (i * tq) // tk

    def block(kb, valid):
        start = pl.multiple_of(kb * tk, tk)
        k = k_ref[pl.ds(start, tk), :].astype(BF16)
        v = v_ref[pl.ds(start, tk), :].astype(BF16)
        z = lax.dot_general(q, k, _NT, preferred_element_type=F32) * scale
        w, carry = _sb_block(z, carry_ref[...], u2, valid)
        carry_ref[...] = carry
        acc_ref[...] += jnp.dot(w.astype(BF16), v, preferred_element_type=F32)

    acc_ref[...] = jnp.zeros_like(acc_ref)
    carry_ref[...] = jnp.zeros_like(carry_ref)
    q_pos = i * tq + (lax.broadcasted_iota(jnp.int32, (rows, tk), 0) & (tq - 1))
    k_pos = diag * tk + lax.broadcasted_iota(jnp.int32, (rows, tk), 1)
    block(diag, k_pos < q_pos)

    def body(s, c):
        block(diag - 1 - s, None)
        return c

    lax.fori_loop(0, diag, body, 0)
    for g in range(groups):
        o_ref[:, g * SB_HEAD_DIM:(g + 1) * SB_HEAD_DIM] = acc_ref[g * tq:(g + 1) * tq, :]


def _sb_prompt(p, batch, seq):
    tq, tk = 128, 256
    nq = seq // tq
    qw = SB_GROUPS * SB_HEAD_DIM
    q_col, k_col, v_col = D_MODEL // qw, (D_MODEL + SB_WIDTH) // SB_HEAD_DIM, (D_MODEL + SB_WIDTH + SB_KV_WIDTH) // SB_HEAD_DIM
    kern = functools.partial(_sb_prompt_kernel, tq=tq, tk=tk)
    return pl.pallas_call(
        kern,
        grid=(batch, SB_KV_HEADS, nq),
        in_specs=[
            pl.BlockSpec((tq, qw), lambda b, h, i: (b * nq + i, q_col + h)),
            pl.BlockSpec((seq, SB_HEAD_DIM), lambda b, h, i: (b, k_col + h)),
            pl.BlockSpec((seq, SB_HEAD_DIM), lambda b, h, i: (b, v_col + h)),
            pl.BlockSpec((2 * tk, tk), lambda b, h, i: (0, 0)),
        ],
        out_specs=pl.BlockSpec((tq, qw), lambda b, h, i: (b * nq + i, h)),
        out_shape=jax.ShapeDtypeStruct((batch * seq, SB_WIDTH), F32),
        scratch_shapes=[pltpu.VMEM((SB_GROUPS * tq, SB_HEAD_DIM), F32), pltpu.VMEM((SB_GROUPS * tq, 1), F32)],
        compiler_params=_params("parallel", "parallel", "parallel"),
        name="sb_prompt_attn",
    )(p, p, p, _suffix_ones(tk))


def _sb_sample_kernel(pt_ref, q_ref, kn_ref, vn_ref, *refs, t_new, n_pages):
    del pt_ref
    pps = PAGES_PER_STEP
    k_refs, v_refs = refs[:pps], refs[pps:2 * pps]
    u_ref, o_ref, acc_ref, carry_ref, kpad_ref, vpad_ref = refs[2 * pps:]
    b, s = pl.program_id(0), pl.program_id(1)
    kvh, hd = SB_KV_HEADS, SB_HEAD_DIM
    rq = SB_GROUPS * t_new
    scale = hd ** -0.5
    u2 = u_ref[...]
    qs = [q_ref[h].astype(BF16) for h in range(kvh)]

    def page(k_page, v_page, valid):
        z = jnp.concatenate(
            [lax.dot_general(qs[h], k_page[:, h * hd:(h + 1) * hd].astype(BF16), _NT, preferred_element_type=F32)
             for h in range(kvh)], axis=0) * scale
        w, carry = _sb_block(z, carry_ref[...], u2, valid)
        carry_ref[...] = carry
        w = w.astype(BF16)
        for h in range(kvh):
            acc_ref[h] += jnp.dot(w[h * rq:(h + 1) * rq], v_page[:, h * hd:(h + 1) * hd].astype(BF16),
                                  preferred_element_type=F32)

    @pl.when(s == 0)
    def _():
        acc_ref[...] = jnp.zeros_like(acc_ref)
        carry_ref[...] = jnp.zeros_like(carry_ref)
        kpad_ref[...] = jnp.zeros_like(kpad_ref)
        vpad_ref[...] = jnp.zeros_like(vpad_ref)
        nb = kn_ref.shape[0]
        kpad_ref[0:nb, :] = kn_ref[...]
        vpad_ref[0:nb, :] = vn_ref[...]
        off = (b % (nb // t_new)) * t_new
        rel = lax.broadcasted_iota(jnp.int32, (kvh * rq, PAGE_SIZE), 1) - off
        tok = _mod(lax.broadcasted_iota(jnp.int32, (kvh * rq, PAGE_SIZE), 0), t_new)
        page(kpad_ref[...], vpad_ref[...], (rel >= 0) & (rel < tok))

    for r in range(pps - 1, -1, -1):
        page(k_refs[r][...], v_refs[r][...], None)

    @pl.when(s == n_pages // pps - 1)
    def _():
        o_ref[...] = acc_ref[...]


def _sb_sample(q_s, p, n_prompt, cache_k, cache_v, layer, page_table):
    db, kvh, rq, hd = q_s.shape
    t_new = rq // SB_GROUPS
    n_pages = page_table.shape[1]
    pps = PAGES_PER_STEP
    steps = n_pages // pps
    nb = 8
    kw = SB_KV_WIDTH
    k_col, v_col = (D_MODEL + SB_WIDTH) // kw, (D_MODEL + SB_WIDTH + kw) // kw
    new_blk = n_prompt // nb
    per = nb // t_new
    ck = cache_k.reshape(cache_k.shape[0], cache_k.shape[1], PAGE_SIZE, kw)
    cv = cache_v.reshape(cache_v.shape[0], cache_v.shape[1], PAGE_SIZE, kw)

    def page_spec(r):
        return pl.BlockSpec((None, None, PAGE_SIZE, kw),
                            lambda b, s, pt: (layer, pt[b, (steps - 1 - s) * pps + r], 0, 0))

    kern = functools.partial(_sb_sample_kernel, t_new=t_new, n_pages=n_pages)
    grid_spec = pltpu.PrefetchScalarGridSpec(
        num_scalar_prefetch=1,
        grid=(db, steps),
        in_specs=[
            pl.BlockSpec((None, kvh, rq, hd), lambda b, s, pt: (b, 0, 0, 0)),
            pl.BlockSpec((nb, kw), lambda b, s, pt: (new_blk + b // per, k_col)),
            pl.BlockSpec((nb, kw), lambda b, s, pt: (new_blk + b // per, v_col)),
        ] + [page_spec(r) for r in range(pps)] + [page_spec(r) for r in range(pps)] + [
            pl.BlockSpec((2 * PAGE_SIZE, PAGE_SIZE), lambda b, s, pt: (0, 0)),
        ],
        out_specs=pl.BlockSpec((None, kvh, rq, hd), lambda b, s, pt: (b, 0, 0, 0)),
        scratch_shapes=[
            pltpu.VMEM((kvh, rq, hd), F32),
            pltpu.VMEM((kvh * rq, 1), F32),
            pltpu.VMEM((PAGE_SIZE, kw), F32),
            pltpu.VMEM((PAGE_SIZE, kw), F32),
        ],
    )
    return pl.pallas_call(
        kern,
        grid_spec=grid_spec,
        out_shape=jax.ShapeDtypeStruct((db, kvh, rq, hd), F32),
        compiler_params=_params("parallel", "arbitrary"),
        name="sb_sample_attn",
    )(page_table, q_s, p, p, *([ck] * pps), *([cv] * pps), _suffix_ones(PAGE_SIZE))


def _lane_halves(x):
    lane = lax.broadcasted_iota(jnp.int32, x.shape, 1)
    lo = lane < ROPE_DIM
    z = jnp.zeros_like(x)
    a0 = jnp.where(lo, x, z)
    b1 = jnp.where(lo, z, x)
    return a0, pltpu.roll(a0, ROPE_DIM, 1), pltpu.roll(b1, ROPE_DIM, 1), b1


def _swa_heads(q, k, v, valid, sinks_ref, sink_base):
    ka0, ka1, kb0, kb1 = _lane_halves(k.astype(BF16))
    va0, va1, vb0, vb1 = _lane_halves(v.astype(BF16))
    outs = []
    for m in range(4):
        qm = q[:, m * LANES:(m + 1) * LANES].astype(BF16)
        k0, k1, v0, v1 = (ka0, ka1, va0, va1) if m < 2 else (kb0, kb1, vb0, vb1)
        o = None
        for half, (kk, vv) in enumerate(((k0, v0), (k1, v1))):
            s = lax.dot_general(qm, kk, _NT, preferred_element_type=F32) * (SWA_HEAD_DIM ** -0.5)
            s = jnp.where(valid, s, NEG_INF)
            sink = sinks_ref[sink_base + 2 * m + half]
            mx = jnp.maximum(jnp.max(s, axis=-1, keepdims=True), sink)
            e = jnp.exp(s - mx)
            pr = e / (jnp.sum(e, axis=-1, keepdims=True) + jnp.exp(sink - mx))
            part = jnp.dot(pr.astype(BF16), vv, preferred_element_type=F32)
            o = part if o is None else o + part
        outs.append(o)
    return jnp.concatenate(outs, axis=1)


def _swa_prompt_kernel(sinks_ref, q_ref, kp_ref, kc_ref, vp_ref, vc_ref, o_ref):
    i, c = pl.program_id(1), pl.program_id(2)
    tq = q_ref.shape[0]
    k = jnp.concatenate([kp_ref[...], kc_ref[...]], axis=0)
    v = jnp.concatenate([vp_ref[...], vc_ref[...]], axis=0)
    qi = lax.broadcasted_iota(jnp.int32, (tq, 2 * tq), 0)
    kj = lax.broadcasted_iota(jnp.int32, (tq, 2 * tq), 1) - tq
    diff = qi - kj
    valid = (diff >= 0) & (diff < WINDOW) & (kj + i * tq >= 0)
    o_ref[...] = _swa_heads(q_ref[...], k, v, valid, sinks_ref, 8 * c)


def _swa_prompt(p, sinks, batch, seq):
    tq = WINDOW
    nq = seq // tq
    qw = 8 * SWA_HEAD_DIM
    q_col = D_MODEL // qw
    k_col = (D_MODEL + SWA_WIDTH) // LANES
    v_col = (D_MODEL + SWA_WIDTH + SWA_KV_WIDTH) // LANES
    grid_spec = pltpu.PrefetchScalarGridSpec(
        num_scalar_prefetch=1,
        grid=(batch, nq, SWA_KV_HEADS // 2),
        in_specs=[
            pl.BlockSpec((tq, qw), lambda b, i, c, sk: (b * nq + i, q_col + c)),
            pl.BlockSpec((tq, LANES), lambda b, i, c, sk: (b * nq + jnp.maximum(i - 1, 0), k_col + c)),
            pl.BlockSpec((tq, LANES), lambda b, i, c, sk: (b * nq + i, k_col + c)),
            pl.BlockSpec((tq, LANES), lambda b, i, c, sk: (b * nq + jnp.maximum(i - 1, 0), v_col + c)),
            pl.BlockSpec((tq, LANES), lambda b, i, c, sk: (b * nq + i, v_col + c)),
        ],
        out_specs=pl.BlockSpec((tq, qw), lambda b, i, c, sk: (b * nq + i, c)),
    )
    return pl.pallas_call(
        _swa_prompt_kernel,
        grid_spec=grid_spec,
        out_shape=jax.ShapeDtypeStruct((batch * seq, SWA_WIDTH), F32),
        compiler_params=_params("parallel", "parallel", "parallel"),
        name="swa_prompt_attn",
    )(sinks, p, p, p, p, p)


def _swa_sample_kernel(sinks_ref, q_ref, kn_ref, vn_ref, ks_ref, vs_ref, o_ref, *, t_new):
    c = pl.program_id(1)
    nb = q_ref.shape[0]
    per = nb // t_new
    w_buf = ks_ref.shape[1]
    k = jnp.concatenate([ks_ref[s] for s in range(per)] + [kn_ref[...]], axis=0)
    v = jnp.concatenate([vs_ref[s] for s in range(per)] + [vn_ref[...]], axis=0)
    nk = per * w_buf + nb
    row = lax.broadcasted_iota(jnp.int32, (nb, nk), 0)
    col = lax.broadcasted_iota(jnp.int32, (nb, nk), 1)
    q_seq, q_tok = _div(row, t_new), _mod(row, t_new)
    in_state = col < per * w_buf
    new_col = jnp.maximum(col - per * w_buf, 0)
    k_seq = jnp.where(in_state, _div(col, w_buf), _div(new_col, t_new))
    k_idx = jnp.where(in_state, _mod(col, w_buf), w_buf + _mod(new_col, t_new))
    diff = w_buf + q_tok - k_idx
    valid = (k_seq == q_seq) & (diff >= 0) & (diff < WINDOW)
    o_ref[...] = _swa_heads(q_ref[...], k, v, valid, sinks_ref, 8 * c)


def _swa_sample(p, sinks, n_prompt, state_k, state_v, t_new):
    db, w_buf, kw = state_k.shape
    nb = 8
    per = nb // t_new
    qw = 8 * SWA_HEAD_DIM
    q_col = D_MODEL // qw
    k_col = (D_MODEL + SWA_WIDTH) // LANES
    v_col = (D_MODEL + SWA_WIDTH + SWA_KV_WIDTH) // LANES
    new_blk = n_prompt // nb
    kern = functools.partial(_swa_sample_kernel, t_new=t_new)
    grid_spec = pltpu.PrefetchScalarGridSpec(
        num_scalar_prefetch=1,
        grid=(db // per, SWA_KV_HEADS // 2),
        in_specs=[
            pl.BlockSpec((nb, qw), lambda g, c, sk: (new_blk + g, q_col + c)),
            pl.BlockSpec((nb, LANES), lambda g, c, sk: (new_blk + g, k_col + c)),
            pl.BlockSpec((nb, LANES), lambda g, c, sk: (new_blk + g, v_col + c)),
            pl.BlockSpec((per, w_buf, LANES), lambda g, c, sk: (g, 0, c)),
            pl.BlockSpec((per, w_buf, LANES), lambda g, c, sk: (g, 0, c)),
        ],
        out_specs=pl.BlockSpec((nb, qw), lambda g, c, sk: (g, c)),
    )
    return pl.pallas_call(
        kern,
        grid_spec=grid_spec,
        out_shape=jax.ShapeDtypeStruct((db * t_new, SWA_WIDTH), F32),
        compiler_params=_params("parallel", "parallel"),
        name="swa_sample_attn",
    )(sinks, p, p, p, state_k, state_v)


def _mla_prep_kernel(p_ref, qn_ref, wuq_ref, kvn_ref, wuk_ref, cos_ref, sin_ref,
                     qlat_ref, qpe_ref, ckv_ref, kpe_ref, ckvb_ref, kpeb_ref):
    c_q = _rms(p_ref[:, 0:MLA_Q_LORA], qn_ref[...]).astype(BF16)
    c_kv = _rms(p_ref[:, MLA_Q_LORA:MLA_Q_LORA + MLA_KV_LORA], kvn_ref[...])
    ckv_ref[...] = c_kv
    ckvb_ref[...] = c_kv.astype(BF16)
    cos, sin = cos_ref[...], sin_ref[...]
    o = MLA_Q_LORA + MLA_KV_LORA
    k_pe = _rope_lanes(p_ref[:, o:o + LANES], cos, sin)
    kpe_ref[...] = k_pe[:, 0:MLA_ROPE]
    lane = lax.broadcasted_iota(jnp.int32, k_pe.shape, 1)
    kpeb_ref[...] = jnp.where(lane < MLA_ROPE, k_pe, 0.0).astype(BF16)
    q = jnp.dot(c_q, wuq_ref[...], preferred_element_type=F32)
    nope_w = MLA_HEADS * MLA_NOPE
    for h in range(MLA_HEADS):
        qn = q[:, h * MLA_NOPE:(h + 1) * MLA_NOPE].astype(BF16)
        qlat_ref[:, h * MLA_KV_LORA:(h + 1) * MLA_KV_LORA] = jnp.dot(
            qn, wuk_ref[h], preferred_element_type=F32).astype(BF16)
    q_pe = _rope_lanes(q[:, nope_w:], cos, sin)
    for pair in range(MLA_HEADS // 2):
        x = q_pe[:, pair * LANES:(pair + 1) * LANES]
        lane = lax.broadcasted_iota(jnp.int32, x.shape, 1)
        first = jnp.where(lane < MLA_ROPE, x, 0.0)
        second = jnp.where(lane < MLA_ROPE, pltpu.roll(x, MLA_ROPE, 1), 0.0)
        qpe_ref[:, (2 * pair) * LANES:(2 * pair + 1) * LANES] = first.astype(BF16)
        qpe_ref[:, (2 * pair + 1) * LANES:(2 * pair + 2) * LANES] = second.astype(BF16)


def _mla_prep(p, q_norm, wuq, kv_norm, wuk, cos, sin):
    n = p.shape[0]
    tm = 256
    pw = p.shape[1] - D_MODEL
    return pl.pallas_call(
        _mla_prep_kernel,
        grid=(n // tm,),
        in_specs=[
            pl.BlockSpec((tm, pw), lambda i: (i, D_MODEL // pw)),
            pl.BlockSpec((1, MLA_Q_LORA), lambda i: (0, 0)),
            pl.BlockSpec(wuq.shape, lambda i: (0, 0)),
            pl.BlockSpec((1, MLA_KV_LORA), lambda i: (0, 0)),
            pl.BlockSpec(wuk.shape, lambda i: (0, 0, 0)),
            pl.BlockSpec((tm, LANES), lambda i: (i, 0)),
            pl.BlockSpec((tm, LANES), lambda i: (i, 0)),
        ],
        out_specs=[
            pl.BlockSpec((tm, MLA_HEADS * MLA_KV_LORA), lambda i: (i, 0)),
            pl.BlockSpec((tm, MLA_HEADS * LANES), lambda i: (i, 0)),
            pl.BlockSpec((tm, MLA_KV_LORA), lambda i: (i, 0)),
            pl.BlockSpec((tm, MLA_ROPE), lambda i: (i, 0)),
            pl.BlockSpec((tm, MLA_KV_LORA), lambda i: (i, 0)),
            pl.BlockSpec((tm, LANES), lambda i: (i, 0)),
        ],
        out_shape=[
            jax.ShapeDtypeStruct((n, MLA_HEADS * MLA_KV_LORA), BF16),
            jax.ShapeDtypeStruct((n, MLA_HEADS * LANES), BF16),
            jax.ShapeDtypeStruct((n, MLA_KV_LORA), F32),
            jax.ShapeDtypeStruct((n, MLA_ROPE), F32),
            jax.ShapeDtypeStruct((n, MLA_KV_LORA), BF16),
            jax.ShapeDtypeStruct((n, LANES), BF16),
        ],
        compiler_params=_params("parallel"),
        name="mla_prep",
    )(p, q_norm.reshape(1, -1), wuq, kv_norm.reshape(1, -1), wuk, cos, sin)


def _softmax_step(s, v, m_ref, l_ref, acc_ref):
    m_old = m_ref[...]
    m_new = jnp.maximum(m_old, jnp.max(s, axis=-1, keepdims=True))
    alpha = jnp.exp(m_old - m_new)
    e = jnp.exp(s - m_new)
    l_ref[...] = alpha * l_ref[...] + jnp.sum(e, axis=-1, keepdims=True)
    acc_ref[...] = alpha * acc_ref[...] + jnp.dot(e.astype(BF16), v, preferred_element_type=F32)
    m_ref[...] = m_new


def _mla_prompt_kernel(ql_ref, qp_ref, ckv_ref, kpe_ref, o_ref, m_ref, l_ref, acc_ref, *, tq, tk):
    i = pl.program_id(1)
    rows = MLA_HEADS * tq
    scale = MLA_QK ** -0.5
    ql, qp = ql_ref[...], qp_ref[...]
    m_ref[...] = jnp.full_like(m_ref, NEG_INF)
    l_ref[...] = jnp.zeros_like(l_ref)
    acc_ref[...] = jnp.zeros_like(acc_ref)
    diag = (i * tq) // tk

    def block(kb, masked):
        start = pl.multiple_of(kb * tk, tk)
        ckv = ckv_ref[pl.ds(start, tk), :]
        s = (lax.dot_general(ql, ckv, _NT, preferred_element_type=F32)
             + lax.dot_general(qp, kpe_ref[pl.ds(start, tk), :], _NT, preferred_element_type=F32)) * scale
        if masked:
            q_pos = i * tq + _div(lax.broadcasted_iota(jnp.int32, (rows, tk), 0), MLA_HEADS)
            k_pos = kb * tk + lax.broadcasted_iota(jnp.int32, (rows, tk), 1)
            s = jnp.where(k_pos <= q_pos, s, NEG_INF)
        _softmax_step(s, ckv, m_ref, l_ref, acc_ref)

    def body(kb, c):
        block(kb, False)
        return c

    lax.fori_loop(0, diag, body, 0)
    block(diag, True)
    o_ref[...] = (acc_ref[...] / l_ref[...]).astype(o_ref.dtype)


def _mla_prompt(q_lat, q_pe, ckv, kpe, batch, seq):
    tq, tk = 32, 256
    nq = seq // tq
    rows = MLA_HEADS * tq
    kern = functools.partial(_mla_prompt_kernel, tq=tq, tk=tk)
    return pl.pallas_call(
        kern,
        grid=(batch, nq),
        in_specs=[
            pl.BlockSpec((rows, MLA_KV_LORA), lambda b, i: (b * nq + i, 0)),
            pl.BlockSpec((rows, LANES), lambda b, i: (b * nq + i, 0)),
            pl.BlockSpec((seq, MLA_KV_LORA), lambda b, i: (b, 0)),
            pl.BlockSpec((seq, LANES), lambda b, i: (b, 0)),
        ],
        out_specs=pl.BlockSpec((rows, MLA_KV_LORA), lambda b, i: (b * nq + i, 0)),
        out_shape=jax.ShapeDtypeStruct((batch * seq * MLA_HEADS, MLA_KV_LORA), BF16),
        scratch_shapes=[pltpu.VMEM((rows, 1), F32), pltpu.VMEM((rows, 1), F32), pltpu.VMEM((rows, MLA_KV_LORA), F32)],
        compiler_params=_params("parallel", "parallel"),
        name="mla_prompt_attn",
    )(q_lat, q_pe, ckv, kpe)


def _mla_sample_kernel(pt_ref, ql_ref, qp_ref, cn_ref, pn_ref, *refs, t_new, n_pages):
    del pt_ref
    pps = PAGES_PER_STEP
    c_refs, p_refs = refs[:pps], refs[pps:2 * pps]
    o_ref, m_ref, l_ref, acc_ref = refs[2 * pps:]
    b, s = pl.program_id(0), pl.program_id(1)
    rows = MLA_HEADS * t_new
    scale = MLA_QK ** -0.5
    ql, qp = ql_ref[...], qp_ref[...]

    @pl.when(s == 0)
    def _():
        nb = cn_ref.shape[0]
        ckv = cn_ref[...]
        sc = (lax.dot_general(ql, ckv, _NT, preferred_element_type=F32)
              + lax.dot_general(qp, pn_ref[...], _NT, preferred_element_type=F32)) * scale
        off = (b % (nb // t_new)) * t_new
        rel = lax.broadcasted_iota(jnp.int32, (rows, nb), 1) - off
        tok = _div(lax.broadcasted_iota(jnp.int32, (rows, nb), 0), MLA_HEADS)
        sc = jnp.where((rel >= 0) & (rel <= tok), sc, NEG_INF)
        m = jnp.max(sc, axis=-1, keepdims=True)
        e = jnp.exp(sc - m)
        m_ref[...] = m
        l_ref[...] = jnp.sum(e, axis=-1, keepdims=True)
        acc_ref[...] = jnp.dot(e.astype(BF16), ckv, preferred_element_type=F32)

    for r in range(pps):
        ckv = c_refs[r][...].astype(BF16)
        kpe = p_refs[r][...].astype(BF16)
        sc = (lax.dot_general(ql, ckv, _NT, preferred_element_type=F32)
              + lax.dot_general(qp[:, 0:MLA_ROPE], kpe, _NT, preferred_element_type=F32)) * scale
        _softmax_step(sc, ckv, m_ref, l_ref, acc_ref)

    @pl.when(s == n_pages // pps - 1)
    def _():
        o_ref[...] = (acc_ref[...] / l_ref[...]).astype(o_ref.dtype)


def _mla_sample(q_lat, q_pe, ckv_new, kpe_new, n_prompt, cache_ckv, cache_kpe, layer, page_table, t_new):
    db, n_pages = page_table.shape
    pps = PAGES_PER_STEP
    steps = n_pages // pps
    rows = MLA_HEADS * t_new
    nb = 16
    per = nb // t_new
    q_blk = n_prompt // t_new
    new_blk = n_prompt // nb

    def page_spec(width, r):
        return pl.BlockSpec((None, None, PAGE_SIZE, width), lambda b, s, pt: (layer, pt[b, s * pps + r], 0, 0))

    kern = functools.partial(_mla_sample_kernel, t_new=t_new, n_pages=n_pages)
    grid_spec = pltpu.PrefetchScalarGridSpec(
        num_scalar_prefetch=1,
        grid=(db, steps),
        in_specs=[
            pl.BlockSpec((rows, MLA_KV_LORA), lambda b, s, pt: (q_blk + b, 0)),
            pl.BlockSpec((rows, LANES), lambda b, s, pt: (q_blk + b, 0)),
            pl.BlockSpec((nb, MLA_KV_LORA), lambda b, s, pt: (new_blk + b // per, 0)),
            pl.BlockSpec((nb, LANES), lambda b, s, pt: (new_blk + b // per, 0)),
        ] + [page_spec(MLA_KV_LORA, r) for r in range(pps)] + [page_spec(MLA_ROPE, r) for r in range(pps)],
        out_specs=pl.BlockSpec((rows, MLA_KV_LORA), lambda b, s, pt: (b, 0)),
        scratch_shapes=[pltpu.VMEM((rows, 1), F32), pltpu.VMEM((rows, 1), F32), pltpu.VMEM((rows, MLA_KV_LORA), F32)],
    )
    return pl.pallas_call(
        kern,
        grid_spec=grid_spec,
        out_shape=jax.ShapeDtypeStruct((db * rows, MLA_KV_LORA), BF16),
        compiler_params=_params("parallel", "arbitrary"),
        name="mla_sample_attn",
    )(page_table, q_lat, q_pe, ckv_new, kpe_new, *([cache_ckv] * pps), *([cache_kpe] * pps))


def _mla_out_kernel(ol_ref, gate_ref, wuv_ref, w_ref, x_ref, g_ref, y_ref):
    o = jnp.concatenate(
        [jnp.dot(ol_ref[:, h * MLA_KV_LORA:(h + 1) * MLA_KV_LORA], wuv_ref[h], preferred_element_type=F32)
         for h in range(MLA_HEADS)], axis=1)
    a = (o * _silu(gate_ref[...])).astype(BF16)
    y = jnp.dot(a, w_ref[...], preferred_element_type=F32)
    y_ref[...] = x_ref[...] + _rms(y, g_ref[...])


def _mla_out(o_lat, p, wuv, w, x, g):
    n, d = x.shape
    tm = 256
    lw = MLA_HEADS * MLA_KV_LORA
    return pl.pallas_call(
        _mla_out_kernel,
        grid=(n // tm,),
        in_specs=[
            pl.BlockSpec((tm, lw), lambda i: (i, 0)),
            pl.BlockSpec((tm, d), lambda i: (i, 0)),
            pl.BlockSpec(wuv.shape, lambda i: (0, 0, 0)),
            pl.BlockSpec((d, d), lambda i: (0, 0)),
            pl.BlockSpec((tm, d), lambda i: (i, 0)),
            pl.BlockSpec((1, d), lambda i: (0, 0)),
        ],
        out_specs=pl.BlockSpec((tm, d), lambda i: (i, 0)),
        out_shape=jax.ShapeDtypeStruct((n, d), F32),
        compiler_params=_params("parallel"),
        name="mla_out_norm",
    )(o_lat, p, wuv, w, x, g.reshape(1, d))


def _rope_tables(pos):
    half = ROPE_DIM // 2
    inv = jnp.float32(ROPE_THETA) ** (-jnp.arange(half, dtype=F32) * 2.0 / ROPE_DIM)
    ang = pos.astype(F32)[:, None] * inv[None, :]
    cos, sin = jnp.cos(ang), jnp.sin(ang)
    cos = jnp.concatenate([cos, cos, cos, cos], axis=1)
    sin = jnp.concatenate([-sin, sin, -sin, sin], axis=1)
    return cos, sin


def _pad_rows(a, n):
    return a if a.shape[0] == n else jnp.concatenate([a, jnp.zeros((n - a.shape[0],) + a.shape[1:], a.dtype)], axis=0)


def kernel(x_prompt, x_sample, cache_sb_k, cache_sb_v, state_swa_k, state_swa_v, cache_mla_ckv, cache_mla_kpe,
           page_table, norm_pre, norm_post, sb_w_in, sb_w_out, swa_w_in, swa_sinks, swa_w_out,
           mla_w_in, mla_q_norm, mla_w_uq, mla_kv_norm, mla_w_uk, mla_w_uv, mla_w_out):
    batch, seq, d = x_prompt.shape
    db, t_new, _ = x_sample.shape
    n_p, n_s = batch * seq, db * t_new
    n = n_p + n_s
    n_pad = -(-n // ROW_TILE) * ROW_TILE
    past = page_table.shape[1] * PAGE_SIZE
    x = _pad_rows(jnp.concatenate([x_prompt.reshape(n_p, d), x_sample.reshape(n_s, d)], axis=0), n_pad)
    pos = jnp.concatenate([jnp.tile(jnp.arange(seq), batch), jnp.tile(past + jnp.arange(t_new), db),
                           jnp.zeros((n_pad - n,), jnp.int32)])
    cos, sin = _rope_tables(pos)

    sb_st, swa_st, mla_st = [], [], []
    for i in range(DEPTH):
        kind, j = i % N_MIXERS, i // N_MIXERS
        if kind == 0:
            qo, ko, vo, go = 0, SB_WIDTH, SB_WIDTH + SB_KV_WIDTH, SB_WIDTH + 2 * SB_KV_WIDTH
            w_in = sb_w_in[j]
            w = jnp.concatenate([w_in[:, go:], w_in[:, qo:ko], w_in[:, ko:vo], w_in[:, vo:go]], axis=1).astype(BF16)
            p = _proj(x, norm_pre[i], w, cos, sin, tn=512)
            o_p = _sb_prompt(p, batch, seq)
            q_s = p[n_p:n, D_MODEL:D_MODEL + SB_WIDTH].reshape(db, t_new, SB_KV_HEADS, SB_GROUPS, SB_HEAD_DIM)
            q_s = q_s.transpose(0, 2, 3, 1, 4).reshape(db, SB_KV_HEADS, SB_GROUPS * t_new, SB_HEAD_DIM)
            o_s = _sb_sample(q_s, p, n_p, cache_sb_k, cache_sb_v, j, page_table)
            o_s = o_s.reshape(db, SB_KV_HEADS, SB_GROUPS, t_new, SB_HEAD_DIM).transpose(0, 3, 1, 2, 4).reshape(n_s, SB_WIDTH)
            o = _pad_rows(jnp.concatenate([o_p, o_s], axis=0), n_pad)
            x = _out(o, p, sb_w_out[j].astype(BF16), x, norm_post[i])
            k_all = p[:, D_MODEL + SB_WIDTH:D_MODEL + SB_WIDTH + SB_KV_WIDTH]
            v_all = p[:, D_MODEL + SB_WIDTH + SB_KV_WIDTH:]
            sb_st.append((k_all[:n_p].reshape(batch, seq, SB_KV_HEADS, SB_HEAD_DIM),
                          v_all[:n_p].reshape(batch, seq, SB_KV_HEADS, SB_HEAD_DIM),
                          k_all[n_p:n].reshape(db, t_new, SB_KV_HEADS, SB_HEAD_DIM),
                          v_all[n_p:n].reshape(db, t_new, SB_KV_HEADS, SB_HEAD_DIM)))
        elif kind == 1:
            qo, ko, vo, go = 0, SWA_WIDTH, SWA_WIDTH + SWA_KV_WIDTH, SWA_WIDTH + 2 * SWA_KV_WIDTH
            w_in = swa_w_in[j]
            w = jnp.concatenate([w_in[:, go:], w_in[:, qo:ko], w_in[:, ko:vo], w_in[:, vo:go]], axis=1).astype(BF16)
            p = _proj(x, norm_pre[i], w, cos, sin, tn=512, rope_cols=(D_MODEL, D_MODEL + SWA_WIDTH + SWA_KV_WIDTH))
            sinks = swa_sinks[j].astype(F32)
            o_p = _swa_prompt(p, sinks, batch, seq)
            w_buf = state_swa_k.shape[2]
            st_k = state_swa_k[j].reshape(db, w_buf, SWA_KV_WIDTH)
            st_v = state_swa_v[j].reshape(db, w_buf, SWA_KV_WIDTH)
            o_s = _swa_sample(p, sinks, n_p, st_k, st_v, t_new)
            o = _pad_rows(jnp.concatenate([o_p, o_s], axis=0), n_pad)
            x = _out(o, p, swa_w_out[j].astype(BF16), x, norm_post[i])
            k_all = p[:, D_MODEL + SWA_WIDTH:D_MODEL + SWA_WIDTH + SWA_KV_WIDTH]
            v_all = p[:, D_MODEL + SWA_WIDTH + SWA_KV_WIDTH:]
            keep = min(WINDOW, seq)
            kp = k_all[:n_p].reshape(batch, seq, SWA_KV_HEADS, SWA_HEAD_DIM)[:, seq - keep:]
            vp = v_all[:n_p].reshape(batch, seq, SWA_KV_HEADS, SWA_HEAD_DIM)[:, seq - keep:]
            ks = jnp.concatenate([st_k, k_all[n_p:n].reshape(db, t_new, SWA_KV_WIDTH)], axis=1)[:, t_new:]
            vs = jnp.concatenate([st_v, v_all[n_p:n].reshape(db, t_new, SWA_KV_WIDTH)], axis=1)[:, t_new:]
            swa_st.append((kp, vp, ks.reshape(db, w_buf, SWA_KV_HEADS, SWA_HEAD_DIM),
                           vs.reshape(db, w_buf, SWA_KV_HEADS, SWA_HEAD_DIM)))
        else:
            w_in = mla_w_in[j]
            o_kv = MLA_Q_LORA + MLA_KV_LORA + MLA_ROPE
            pad = jnp.zeros((d, 1024 - o_kv), w_in.dtype)
            w = jnp.concatenate([w_in[:, o_kv:], w_in[:, :o_kv], pad], axis=1).astype(BF16)
            p = _proj(x, norm_pre[i], w, cos, sin, tn=512)
            wuq = mla_w_uq[j].reshape(MLA_Q_LORA, MLA_HEADS, MLA_QK)
            wuq = jnp.concatenate([wuq[:, :, :MLA_NOPE].reshape(MLA_Q_LORA, -1),
                                   wuq[:, :, MLA_NOPE:].reshape(MLA_Q_LORA, -1)], axis=1).astype(BF16)
            wuk = mla_w_uk[j].transpose(1, 2, 0).astype(BF16)
            wuv = mla_w_uv[j].transpose(1, 0, 2).astype(BF16)
            q_lat, q_pe, ckv, kpe, ckv_b, kpe_b = _mla_prep(p, mla_q_norm[j], wuq, mla_kv_norm[j], wuk, cos, sin)
            q_lat = q_lat.reshape(n_pad * MLA_HEADS, MLA_KV_LORA)
            q_pe = q_pe.reshape(n_pad * MLA_HEADS, LANES)
            o_p = _mla_prompt(q_lat, q_pe, ckv_b, kpe_b, batch, seq)
            o_s = _mla_sample(q_lat, q_pe, ckv_b, kpe_b, n_p, cache_mla_ckv, cache_mla_kpe, j, page_table, t_new)
            o = jnp.concatenate([o_p, o_s], axis=0).reshape(n, MLA_HEADS * MLA_KV_LORA)
            x = _mla_out(_pad_rows(o, n_pad), p, wuv, mla_w_out[j].astype(BF16), x, norm_post[i])
            mla_st.append((ckv[:n_p].reshape(batch, seq, MLA_KV_LORA), kpe[:n_p].reshape(batch, seq, MLA_ROPE),
                           ckv[n_p:n].reshape(db, t_new, MLA_KV_LORA), kpe[n_p:n].reshape(db, t_new, MLA_ROPE)))

    def stack(st, k):
        return jnp.stack([s[k] for s in st])

    return (x[:n_p].reshape(batch, seq, d), x[n_p:n].reshape(db, t_new, d),
            stack(sb_st, 0), stack(sb_st, 1), stack(sb_st, 2), stack(sb_st, 3),
            stack(swa_st, 0), stack(swa_st, 1), stack(swa_st, 2), stack(swa_st, 3),
            stack(mla_st, 0), stack(mla_st, 1), stack(mla_st, 2), stack(mla_st, 3))
```

```python
import functools
import math

import jax
import jax.numpy as jnp
from jax import lax
from jax.experimental import pallas as pl
from jax.experimental.pallas import tpu as pltpu

F32 = jnp.float32
BF16 = jnp.bfloat16

D_MODEL = 2048
DEPTH = 4
N_MIXERS = 3
PAGE_SIZE = 128
ROPE_THETA = 10000.0
NORM_EPS = 1e-6
NEG_INF = -1e30
LOG2E = math.log2(math.e)

SB_HEADS = 16
SB_KV_HEADS = 4
SB_HEAD_DIM = D_MODEL // SB_HEADS
SB_GROUPS = SB_HEADS // SB_KV_HEADS
SB_WIDTH = SB_HEADS * SB_HEAD_DIM
SB_KV_WIDTH = SB_KV_HEADS * SB_HEAD_DIM

SWA_HEADS = 32
SWA_KV_HEADS = 8
SWA_HEAD_DIM = D_MODEL // SWA_HEADS
SWA_GROUPS = SWA_HEADS // SWA_KV_HEADS
SWA_WIDTH = SWA_HEADS * SWA_HEAD_DIM
SWA_KV_WIDTH = SWA_KV_HEADS * SWA_HEAD_DIM
WINDOW = 128

MLA_HEADS = 16
MLA_Q_LORA = D_MODEL // 4
MLA_KV_LORA = D_MODEL // 8
MLA_NOPE = 128
MLA_ROPE = 64
MLA_QK = MLA_NOPE + MLA_ROPE
MLA_V = 128
MLA_WIDTH = MLA_HEADS * MLA_V

LANES = 128
ROPE_DIM = 64
VMEM_LIMIT = 56 * 1024 * 1024

ROW_TILE = 512
PAGES_PER_STEP = 16
MLA_PAGES_PER_STEP = 32
MLA_TQ = 64
MLA_QW = MLA_KV_LORA + LANES

_NT = (((1,), (1,)), ((), ()))


def _params(*sem):
    return pltpu.CompilerParams(dimension_semantics=sem, vmem_limit_bytes=VMEM_LIMIT)


def _div(x, n):
    assert n & (n - 1) == 0
    return x >> (n.bit_length() - 1)


def _mod(x, n):
    assert n & (n - 1) == 0
    return x & (n - 1)


def _tile_lanes(x, width):
    reps = width // LANES
    return x if reps == 1 else jnp.concatenate([x] * reps, axis=1)


def _silu(x):
    return x / (1.0 + jnp.exp(-x))


def _rms(x, g):
    return x * lax.rsqrt(jnp.mean(x * x, axis=-1, keepdims=True) + NORM_EPS) * g


def _rope_lanes(x, cos, sin_signed):
    w = x.shape[1]
    cos, sin_signed = _tile_lanes(cos, w), _tile_lanes(sin_signed, w)
    lane = lax.broadcasted_iota(jnp.int32, x.shape, 1)
    first_half = (lane & (ROPE_DIM - 1)) < (ROPE_DIM // 2)
    partner = jnp.where(first_half, pltpu.roll(x, w - ROPE_DIM // 2, 1), pltpu.roll(x, ROPE_DIM // 2, 1))
    return x * cos + partner * sin_signed


def _proj_kernel(x_ref, g_ref, w_ref, cos_ref, sin_ref, o_ref, h_ref, *, rope_lo, rope_hi):
    j = pl.program_id(1)

    @pl.when(j == 0)
    def _():
        h_ref[...] = _rms(x_ref[...], g_ref[...]).astype(BF16)

    acc = jnp.dot(h_ref[...], w_ref[...], preferred_element_type=F32)
    if rope_hi > rope_lo:
        in_rope = (j >= rope_lo) & (j < rope_hi)

        @pl.when(in_rope)
        def _():
            o_ref[...] = _rope_lanes(acc, cos_ref[...], sin_ref[...])

        @pl.when(jnp.logical_not(in_rope))
        def _():
            o_ref[...] = acc
    else:
        o_ref[...] = acc


def _proj(x, g, w, cos, sin, *, tn, rope_cols=(0, 0)):
    n, d = x.shape
    c = w.shape[1]
    tm = ROW_TILE
    kern = functools.partial(_proj_kernel, rope_lo=rope_cols[0] // tn, rope_hi=rope_cols[1] // tn)
    return pl.pallas_call(
        kern,
        grid=(n // tm, c // tn),
        in_specs=[
            pl.BlockSpec((tm, d), lambda i, j: (i, 0)),
            pl.BlockSpec((1, d), lambda i, j: (0, 0)),
            pl.BlockSpec((d, tn), lambda i, j: (0, j)),
            pl.BlockSpec((tm, LANES), lambda i, j: (i, 0)),
            pl.BlockSpec((tm, LANES), lambda i, j: (i, 0)),
        ],
        out_specs=pl.BlockSpec((tm, tn), lambda i, j: (i, j)),
        out_shape=jax.ShapeDtypeStruct((n, c), F32),
        scratch_shapes=[pltpu.VMEM((tm, d), BF16)],
        compiler_params=_params("parallel", "arbitrary"),
        name="norm_proj",
    )(x, g.reshape(1, d), w, cos, sin)


def _out_kernel(o_ref, gate_ref, w_ref, x_ref, g_ref, y_ref):
    a = (o_ref[...].astype(F32) * _silu(gate_ref[...])).astype(BF16)
    y = jnp.dot(a, w_ref[...], preferred_element_type=F32)
    y_ref[...] = x_ref[...] + _rms(y, g_ref[...])


def _out(o, p, w, x, g):
    n, d = x.shape
    tm = 256
    return pl.pallas_call(
        _out_kernel,
        grid=(n // tm,),
        in_specs=[
            pl.BlockSpec((tm, d), lambda i: (i, 0)),
            pl.BlockSpec((tm, d), lambda i: (i, 0)),
            pl.BlockSpec((d, d), lambda i: (0, 0)),
            pl.BlockSpec((tm, d), lambda i: (i, 0)),
            pl.BlockSpec((1, d), lambda i: (0, 0)),
        ],
        out_specs=pl.BlockSpec((tm, d), lambda i: (i, 0)),
        out_shape=jax.ShapeDtypeStruct((n, d), F32),
        compiler_params=_params("parallel"),
        name="gate_out_norm",
    )(o, p, w, x, g.reshape(1, d))


def _suffix_ones():
    s = lax.broadcasted_iota(jnp.int32, (LANES, LANES), 0)
    j = lax.broadcasted_iota(jnp.int32, (LANES, LANES), 1)
    u = (s > j).astype(BF16)
    return jnp.concatenate([u, u], axis=0)


def _sb_weights(t, carry, u2, valid):
    rows, width = t.shape
    nb = width // LANES
    drop = jnp.maximum(t, 0.0) + jnp.log(1.0 + jnp.exp2(-jnp.abs(t))) * LOG2E
    if valid is not None:
        drop = jnp.where(valid, drop, 0.0)
    hi = drop.astype(BF16)
    lo = (drop - hi.astype(F32)).astype(BF16)
    blocks = [jnp.concatenate([hi[:, p * LANES:(p + 1) * LANES], lo[:, p * LANES:(p + 1) * LANES]], axis=1)
              for p in range(nb)]
    later = jnp.dot(jnp.concatenate(blocks, axis=0), u2, preferred_element_type=F32)
    cols = [None] * nb
    for p in reversed(range(nb)):
        cols[p] = later[p * rows:(p + 1) * rows] + carry
        carry = carry + jnp.sum(drop[:, p * LANES:(p + 1) * LANES], axis=1, keepdims=True)
    w = jnp.exp2(t - drop - jnp.concatenate(cols, axis=1))
    if valid is not None:
        w = jnp.where(valid, w, 0.0)
    return w, carry


def _sb_prompt_kernel(q_ref, k_ref, v_ref, u_ref, o_ref, acc_ref, carry_ref, *, tq, tk):
    i = pl.program_id(2)
    groups = SB_GROUPS
    rows = groups * tq
    c = (SB_HEAD_DIM ** -0.5) * LOG2E
    q = jnp.concatenate([q_ref[:, g * SB_HEAD_DIM:(g + 1) * SB_HEAD_DIM] for g in range(groups)], axis=0)
    q = (q * c).astype(BF16)
    u2 = u_ref[...]
    diag = (i * tq) // tk

    def block(kb, valid):
        start = pl.multiple_of(kb * tk, tk)
        k = k_ref[pl.ds(start, tk), :].astype(BF16)
        v = v_ref[pl.ds(start, tk), :].astype(BF16)
        t = lax.dot_general(q, k, _NT, preferred_element_type=F32)
        w, carry = _sb_weights(t, carry_ref[...], u2, valid)
        carry_ref[...] = carry
        acc_ref[...] += jnp.dot(w.astype(BF16), v, preferred_element_type=F32)

    acc_ref[...] = jnp.zeros_like(acc_ref)
    carry_ref[...] = jnp.zeros_like(carry_ref)
    q_pos = i * tq + _mod(lax.broadcasted_iota(jnp.int32, (rows, tk), 0), tq)
    k_pos = diag * tk + lax.broadcasted_iota(jnp.int32, (rows, tk), 1)
    block(diag, k_pos < q_pos)

    def body(s, carry):
        block(diag - 1 - s, None)
        return carry

    lax.fori_loop(0, diag, body, 0)
    for g in range(groups):
        o_ref[:, g * SB_HEAD_DIM:(g + 1) * SB_HEAD_DIM] = acc_ref[g * tq:(g + 1) * tq, :]


def _sb_prompt(p, batch, seq):
    tq, tk = 128, 512
    nq = seq // tq
    qw = SB_GROUPS * SB_HEAD_DIM
    q_col, k_col, v_col = D_MODEL // qw, (D_MODEL + SB_WIDTH) // SB_HEAD_DIM, (D_MODEL + SB_WIDTH + SB_KV_WIDTH) // SB_HEAD_DIM
    kern = functools.partial(_sb_prompt_kernel, tq=tq, tk=tk)
    return pl.pallas_call(
        kern,
        grid=(batch, SB_KV_HEADS, nq),
        in_specs=[
            pl.BlockSpec((tq, qw), lambda b, h, i: (b * nq + i, q_col + h)),
            pl.BlockSpec((seq, SB_HEAD_DIM), lambda b, h, i: (b, k_col + h)),
            pl.BlockSpec((seq, SB_HEAD_DIM), lambda b, h, i: (b, v_col + h)),
            pl.BlockSpec((2 * LANES, LANES), lambda b, h, i: (0, 0)),
        ],
        out_specs=pl.BlockSpec((tq, qw), lambda b, h, i: (b * nq + i, h)),
        out_shape=jax.ShapeDtypeStruct((batch * seq, SB_WIDTH), F32),
        scratch_shapes=[pltpu.VMEM((SB_GROUPS * tq, SB_HEAD_DIM), F32), pltpu.VMEM((SB_GROUPS * tq, LANES), F32)],
        compiler_params=_params("parallel", "parallel", "parallel"),
        name="sb_prompt_attn",
    )(p, p, p, _suffix_ones())


def _sb_sample_kernel(pt_ref, q_ref, kn_ref, vn_ref, *refs, t_new, pps, n_steps):
    del pt_ref
    k_refs, v_refs = refs[:pps], refs[pps:2 * pps]
    u_ref, o_ref, acc_ref, carry_ref, kpad_ref, vpad_ref = refs[2 * pps:]
    b, s = pl.program_id(0), pl.program_id(1)
    kvh, hd = SB_KV_HEADS, SB_HEAD_DIM
    rq = SB_GROUPS * t_new
    c = (hd ** -0.5) * LOG2E
    u2 = u_ref[...]
    qs = [(q_ref[h] * c).astype(BF16) for h in range(kvh)]

    def attend(k_heads, v_heads, valid):
        t = jnp.concatenate([lax.dot_general(qs[h], k_heads[h], _NT, preferred_element_type=F32)
                             for h in range(kvh)], axis=0)
        w, carry = _sb_weights(t, carry_ref[...], u2, valid)
        carry_ref[...] = carry
        w = w.astype(BF16)
        for h in range(kvh):
            acc_ref[h] += jnp.dot(w[h * rq:(h + 1) * rq], v_heads[h], preferred_element_type=F32)

    @pl.when(s == 0)
    def _():
        acc_ref[...] = jnp.zeros_like(acc_ref)
        carry_ref[...] = jnp.zeros_like(carry_ref)
        kpad_ref[...] = jnp.zeros_like(kpad_ref)
        vpad_ref[...] = jnp.zeros_like(vpad_ref)
        nb = kn_ref.shape[0]
        kpad_ref[0:nb, :] = kn_ref[...]
        vpad_ref[0:nb, :] = vn_ref[...]
        off = (b % (nb // t_new)) * t_new
        rel = lax.broadcasted_iota(jnp.int32, (kvh * rq, PAGE_SIZE), 1) - off
        tok = _mod(lax.broadcasted_iota(jnp.int32, (kvh * rq, PAGE_SIZE), 0), t_new)
        attend([kpad_ref[:, h * hd:(h + 1) * hd].astype(BF16) for h in range(kvh)],
               [vpad_ref[:, h * hd:(h + 1) * hd].astype(BF16) for h in range(kvh)],
               (rel >= 0) & (rel < tok))

    def head_rows(page_refs, h):
        return jnp.concatenate([r[pl.ds(h, PAGE_SIZE, stride=kvh), :] for r in page_refs], axis=0).astype(BF16)

    attend([head_rows(k_refs, h) for h in range(kvh)], [head_rows(v_refs, h) for h in range(kvh)], None)

    @pl.when(s == n_steps - 1)
    def _():
        o_ref[...] = acc_ref[...]


def _sb_sample(q_s, p, n_prompt, cache_k, cache_v, layer, page_table):
    db, kvh, rq, hd = q_s.shape
    t_new = rq // SB_GROUPS
    n_pages = page_table.shape[1]
    pps = math.gcd(PAGES_PER_STEP, n_pages)
    steps = n_pages // pps
    nb = 8
    kw = SB_KV_WIDTH
    k_col, v_col = (D_MODEL + SB_WIDTH) // kw, (D_MODEL + SB_WIDTH + kw) // kw
    new_blk = n_prompt // nb
    per = nb // t_new
    ck = cache_k.reshape(cache_k.shape[0], cache_k.shape[1], PAGE_SIZE * kvh, hd)
    cv = cache_v.reshape(cache_v.shape[0], cache_v.shape[1], PAGE_SIZE * kvh, hd)

    def page_spec(r):
        return pl.BlockSpec((None, None, PAGE_SIZE * kvh, hd),
                            lambda b, s, pt: (layer, pt[b, (steps - 1 - s) * pps + r], 0, 0))

    kern = functools.partial(_sb_sample_kernel, t_new=t_new, pps=pps, n_steps=steps)
    grid_spec = pltpu.PrefetchScalarGridSpec(
        num_scalar_prefetch=1,
        grid=(db, steps),
        in_specs=[
            pl.BlockSpec((None, kvh, rq, hd), lambda b, s, pt: (b, 0, 0, 0)),
            pl.BlockSpec((nb, kw), lambda b, s, pt: (new_blk + b // per, k_col)),
            pl.BlockSpec((nb, kw), lambda b, s, pt: (new_blk + b // per, v_col)),
        ] + [page_spec(r) for r in range(pps)] + [page_spec(r) for r in range(pps)] + [
            pl.BlockSpec((2 * LANES, LANES), lambda b, s, pt: (0, 0)),
        ],
        out_specs=pl.BlockSpec((None, kvh, rq, hd), lambda b, s, pt: (b, 0, 0, 0)),
        scratch_shapes=[
            pltpu.VMEM((kvh, rq, hd), F32),
            pltpu.VMEM((kvh * rq, LANES), F32),
            pltpu.VMEM((PAGE_SIZE, kw), F32),
            pltpu.VMEM((PAGE_SIZE, kw), F32),
        ],
    )
    return pl.pallas_call(
        kern,
        grid_spec=grid_spec,
        out_shape=jax.ShapeDtypeStruct((db, kvh, rq, hd), F32),
        compiler_params=_params("parallel", "arbitrary"),
        name="sb_sample_attn",
    )(page_table, q_s, p, p, *([ck] * pps), *([cv] * pps), _suffix_ones())


def _lane_halves(x):
    lane = lax.broadcasted_iota(jnp.int32, x.shape, 1)
    lo = lane < ROPE_DIM
    z = jnp.zeros_like(x)
    a0 = jnp.where(lo, x, z)
    b1 = jnp.where(lo, z, x)
    return a0, pltpu.roll(a0, ROPE_DIM, 1), pltpu.roll(b1, ROPE_DIM, 1), b1


def _swa_heads(q, k, v, valid, sinks_ref, sink_base):
    ka0, ka1, kb0, kb1 = _lane_halves(k.astype(BF16))
    va0, va1, vb0, vb1 = _lane_halves(v.astype(BF16))
    outs = []
    for m in range(4):
        qm = q[:, m * LANES:(m + 1) * LANES].astype(BF16)
        k0, k1, v0, v1 = (ka0, ka1, va0, va1) if m < 2 else (kb0, kb1, vb0, vb1)
        o = None
        for half, (kk, vv) in enumerate(((k0, v0), (k1, v1))):
            s = lax.dot_general(qm, kk, _NT, preferred_element_type=F32) * (SWA_HEAD_DIM ** -0.5)
            s = jnp.where(valid, s, NEG_INF)
            sink = sinks_ref[sink_base + 2 * m + half]
            mx = jnp.maximum(jnp.max(s, axis=-1, keepdims=True), sink)
            e = jnp.exp(s - mx)
            pr = e / (jnp.sum(e, axis=-1, keepdims=True) + jnp.exp(sink - mx))
            part = jnp.dot(pr.astype(BF16), vv, preferred_element_type=F32)
            o = part if o is None else o + part
        outs.append(o)
    return jnp.concatenate(outs, axis=1)


def _swa_prompt_kernel(sinks_ref, q_ref, kp_ref, kc_ref, vp_ref, vc_ref, o_ref):
    i, c = pl.program_id(1), pl.program_id(2)
    tq = q_ref.shape[0]
    k = jnp.concatenate([kp_ref[...], kc_ref[...]], axis=0)
    v = jnp.concatenate([vp_ref[...], vc_ref[...]], axis=0)
    qi = lax.broadcasted_iota(jnp.int32, (tq, 2 * tq), 0)
    kj = lax.broadcasted_iota(jnp.int32, (tq, 2 * tq), 1) - tq
    diff = qi - kj
    valid = (diff >= 0) & (diff < WINDOW) & (kj + i * tq >= 0)
    o_ref[...] = _swa_heads(q_ref[...], k, v, valid, sinks_ref, 8 * c)


def _swa_prompt(p, sinks, batch, seq):
    tq = WINDOW
    nq = seq // tq
    qw = 8 * SWA_HEAD_DIM
    q_col = D_MODEL // qw
    k_col = (D_MODEL + SWA_WIDTH) // LANES
    v_col = (D_MODEL + SWA_WIDTH + SWA_KV_WIDTH) // LANES
    grid_spec = pltpu.PrefetchScalarGridSpec(
        num_scalar_prefetch=1,
        grid=(batch, nq, SWA_KV_HEADS // 2),
        in_specs=[
            pl.BlockSpec((tq, qw), lambda b, i, c, sk: (b * nq + i, q_col + c)),
            pl.BlockSpec((tq, LANES), lambda b, i, c, sk: (b * nq + jnp.maximum(i - 1, 0), k_col + c)),
            pl.BlockSpec((tq, LANES), lambda b, i, c, sk: (b * nq + i, k_col + c)),
            pl.BlockSpec((tq, LANES), lambda b, i, c, sk: (b * nq + jnp.maximum(i - 1, 0), v_col + c)),
            pl.BlockSpec((tq, LANES), lambda b, i, c, sk: (b * nq + i, v_col + c)),
        ],
        out_specs=pl.BlockSpec((tq, qw), lambda b, i, c, sk: (b * nq + i, c)),
    )
    return pl.pallas_call(
        _swa_prompt_kernel,
        grid_spec=grid_spec,
        out_shape=jax.ShapeDtypeStruct((batch * seq, SWA_WIDTH), F32),
        compiler_params=_params("parallel", "parallel", "parallel"),
        name="swa_prompt_attn",
    )(sinks, p, p, p, p, p)


def _swa_sample_kernel(sinks_ref, q_ref, kn_ref, vn_ref, ks_ref, vs_ref, o_ref, *, t_new):
    c = pl.program_id(1)
    nb = q_ref.shape[0]
    per = nb // t_new
    w_buf = ks_ref.shape[1]
    k = jnp.concatenate([ks_ref[s] for s in range(per)] + [kn_ref[...]], axis=0)
    v = jnp.concatenate([vs_ref[s] for s in range(per)] + [vn_ref[...]], axis=0)
    nk = per * w_buf + nb
    row = lax.broadcasted_iota(jnp.int32, (nb, nk), 0)
    col = lax.broadcasted_iota(jnp.int32, (nb, nk), 1)
    q_seq, q_tok = _div(row, t_new), _mod(row, t_new)
    in_state = col < per * w_buf
    new_col = jnp.maximum(col - per * w_buf, 0)
    k_seq = jnp.where(in_state, _div(col, w_buf), _div(new_col, t_new))
    k_idx = jnp.where(in_state, _mod(col, w_buf), w_buf + _mod(new_col, t_new))
    diff = w_buf + q_tok - k_idx
    valid = (k_seq == q_seq) & (diff >= 0) & (diff < WINDOW)
    o_ref[...] = _swa_heads(q_ref[...], k, v, valid, sinks_ref, 8 * c)


def _swa_sample(p, sinks, n_prompt, state_k, state_v, t_new):
    db, w_buf, kw = state_k.shape
    nb = 8
    per = nb // t_new
    qw = 8 * SWA_HEAD_DIM
    q_col = D_MODEL // qw
    k_col = (D_MODEL + SWA_WIDTH) // LANES
    v_col = (D_MODEL + SWA_WIDTH + SWA_KV_WIDTH) // LANES
    new_blk = n_prompt // nb
    kern = functools.partial(_swa_sample_kernel, t_new=t_new)
    grid_spec = pltpu.PrefetchScalarGridSpec(
        num_scalar_prefetch=1,
        grid=(db // per, SWA_KV_HEADS // 2),
        in_specs=[
            pl.BlockSpec((nb, qw), lambda g, c, sk: (new_blk + g, q_col + c)),
            pl.BlockSpec((nb, LANES), lambda g, c, sk: (new_blk + g, k_col + c)),
            pl.BlockSpec((nb, LANES), lambda g, c, sk: (new_blk + g, v_col + c)),
            pl.BlockSpec((per, w_buf, LANES), lambda g, c, sk: (g, 0, c)),
            pl.BlockSpec((per, w_buf, LANES), lambda g, c, sk: (g, 0, c)),
        ],
        out_specs=pl.BlockSpec((nb, qw), lambda g, c, sk: (g, c)),
    )
    return pl.pallas_call(
        kern,
        grid_spec=grid_spec,
        out_shape=jax.ShapeDtypeStruct((db * t_new, SWA_WIDTH), F32),
        compiler_params=_params("parallel", "parallel"),
        name="swa_sample_attn",
    )(sinks, p, p, p, state_k, state_v)


def _mla_prep_kernel(p_ref, qn_ref, wuq_ref, kvn_ref, wuk_ref, cos_ref, sin_ref,
                     qcat_ref, ckv_ref, kpe_ref, kcat_ref):
    tm = p_ref.shape[0]
    c = (MLA_QK ** -0.5) * LOG2E
    c_q = _rms(p_ref[:, 0:MLA_Q_LORA], qn_ref[...]).astype(BF16)
    c_kv = _rms(p_ref[:, MLA_Q_LORA:MLA_Q_LORA + MLA_KV_LORA], kvn_ref[...])
    cos, sin = cos_ref[...], sin_ref[...]
    o = MLA_Q_LORA + MLA_KV_LORA
    k_pe = _rope_lanes(p_ref[:, o:o + LANES], cos, sin)
    ckv_ref[...] = c_kv
    kpe_ref[...] = k_pe[:, 0:MLA_ROPE]
    kcat_ref[:, 0:MLA_KV_LORA] = c_kv.astype(BF16)
    kcat_ref[:, MLA_KV_LORA:] = k_pe.astype(BF16)
    q = jnp.dot(c_q, wuq_ref[...], preferred_element_type=F32)
    q_pe = _rope_lanes(q[:, MLA_HEADS * MLA_NOPE:], cos, sin) * c
    lane = lax.broadcasted_iota(jnp.int32, (tm, LANES), 1)
    for h in range(MLA_HEADS):
        qn = q[:, h * MLA_NOPE:(h + 1) * MLA_NOPE].astype(BF16)
        q_lat = (jnp.dot(qn, wuk_ref[h], preferred_element_type=F32) * c).astype(BF16)
        x = q_pe[:, (h // 2) * LANES:(h // 2 + 1) * LANES]
        if h % 2:
            x = pltpu.roll(x, MLA_ROPE, 1)
        pe = jnp.where(lane < MLA_ROPE, x, 0.0).astype(BF16)
        for tile in range(tm // MLA_TQ):
            r = slice(tile * MLA_TQ, (tile + 1) * MLA_TQ)
            qcat_ref[tile, h, :, 0:MLA_KV_LORA] = q_lat[r]
            qcat_ref[tile, h, :, MLA_KV_LORA:] = pe[r]


def _mla_prep(p, q_norm, wuq, kv_norm, wuk, cos, sin):
    n = p.shape[0]
    tm = 256
    pw = p.shape[1] - D_MODEL
    return pl.pallas_call(
        _mla_prep_kernel,
        grid=(n // tm,),
        in_specs=[
            pl.BlockSpec((tm, pw), lambda i: (i, D_MODEL // pw)),
            pl.BlockSpec((1, MLA_Q_LORA), lambda i: (0, 0)),
            pl.BlockSpec(wuq.shape, lambda i: (0, 0)),
            pl.BlockSpec((1, MLA_KV_LORA), lambda i: (0, 0)),
            pl.BlockSpec(wuk.shape, lambda i: (0, 0, 0)),
            pl.BlockSpec((tm, LANES), lambda i: (i, 0)),
            pl.BlockSpec((tm, LANES), lambda i: (i, 0)),
        ],
        out_specs=[
            pl.BlockSpec((tm // MLA_TQ, MLA_HEADS, MLA_TQ, MLA_QW), lambda i: (i, 0, 0, 0)),
            pl.BlockSpec((tm, MLA_KV_LORA), lambda i: (i, 0)),
            pl.BlockSpec((tm, MLA_ROPE), lambda i: (i, 0)),
            pl.BlockSpec((tm, MLA_QW), lambda i: (i, 0)),
        ],
        out_shape=[
            jax.ShapeDtypeStruct((n // MLA_TQ, MLA_HEADS, MLA_TQ, MLA_QW), BF16),
            jax.ShapeDtypeStruct((n, MLA_KV_LORA), F32),
            jax.ShapeDtypeStruct((n, MLA_ROPE), F32),
            jax.ShapeDtypeStruct((n, MLA_QW), BF16),
        ],
        compiler_params=_params("parallel"),
        name="mla_prep",
    )(p, q_norm.reshape(1, -1), wuq, kv_norm.reshape(1, -1), wuk, cos, sin)


def _softmax_step(s, v, m_ref, l_ref, acc_ref):
    m_old = m_ref[...]
    m_new = jnp.maximum(m_old, jnp.max(s, axis=1, keepdims=True))
    alpha = jnp.exp2(m_old - m_new)
    e = jnp.exp2(s - _tile_lanes(m_new, s.shape[1]))
    l_ref[...] = alpha * l_ref[...] + jnp.sum(e, axis=1, keepdims=True)
    acc_ref[...] = _tile_lanes(alpha, acc_ref.shape[1]) * acc_ref[...] + jnp.dot(
        e.astype(BF16), v, preferred_element_type=F32)
    m_ref[...] = m_new


def _mla_prompt_kernel(q_ref, k_ref, o_ref, m_ref, l_ref, acc_ref, *, tk):
    i = pl.program_id(1)
    tq = MLA_TQ
    rows = MLA_HEADS * tq
    q = jnp.concatenate([q_ref[h] for h in range(MLA_HEADS)], axis=0)
    m_ref[...] = jnp.full_like(m_ref, NEG_INF)
    l_ref[...] = jnp.zeros_like(l_ref)
    acc_ref[...] = jnp.zeros_like(acc_ref)
    diag = (i * tq) // tk

    def block(kb, masked):
        start = pl.multiple_of(kb * tk, tk)
        kc = k_ref[pl.ds(start, tk), :]
        s = lax.dot_general(q, kc, _NT, preferred_element_type=F32)
        if masked:
            q_pos = i * tq + _mod(lax.broadcasted_iota(jnp.int32, (rows, tk), 0), tq)
            k_pos = kb * tk + lax.broadcasted_iota(jnp.int32, (rows, tk), 1)
            s = jnp.where(k_pos <= q_pos, s, NEG_INF)
        _softmax_step(s, kc[:, 0:MLA_KV_LORA], m_ref, l_ref, acc_ref)

    def body(kb, carry):
        block(kb, False)
        return carry

    lax.fori_loop(0, diag, body, 0)
    block(diag, True)
    o = (acc_ref[...] / _tile_lanes(l_ref[...], MLA_KV_LORA)).astype(o_ref.dtype)
    for h in range(MLA_HEADS):
        o_ref[h] = o[h * tq:(h + 1) * tq]


def _mla_prompt(q_cat, k_cat, batch, seq):
    tk = 512
    nq = seq // MLA_TQ
    rows = MLA_HEADS * MLA_TQ
    kern = functools.partial(_mla_prompt_kernel, tk=tk)
    return pl.pallas_call(
        kern,
        grid=(batch, nq),
        in_specs=[
            pl.BlockSpec((None, MLA_HEADS, MLA_TQ, MLA_QW), lambda b, i: (b * nq + i, 0, 0, 0)),
            pl.BlockSpec((seq, MLA_QW), lambda b, i: (b, 0)),
        ],
        out_specs=pl.BlockSpec((None, MLA_HEADS, MLA_TQ, MLA_KV_LORA), lambda b, i: (b * nq + i, 0, 0, 0)),
        out_shape=jax.ShapeDtypeStruct((batch * nq, MLA_HEADS, MLA_TQ, MLA_KV_LORA), BF16),
        scratch_shapes=[pltpu.VMEM((rows, LANES), F32), pltpu.VMEM((rows, LANES), F32),
                        pltpu.VMEM((rows, MLA_KV_LORA), F32)],
        compiler_params=_params("parallel", "parallel"),
        name="mla_prompt_attn",
    )(q_cat, k_cat)


def _mla_sample_kernel(pt_ref, q_ref, kn_ref, *refs, t_new, pps, n_steps):
    del pt_ref
    c_refs, p_refs = refs[:pps], refs[pps:2 * pps]
    o_ref, m_ref, l_ref, acc_ref = refs[2 * pps:]
    b, s = pl.program_id(0), pl.program_id(1)
    rows = MLA_HEADS * t_new
    q = q_ref[...]

    @pl.when(s == 0)
    def _():
        nb = kn_ref.shape[0]
        kn = kn_ref[...]
        sc = lax.dot_general(q, kn, _NT, preferred_element_type=F32)
        off = (b % (nb // t_new)) * t_new
        rel = lax.broadcasted_iota(jnp.int32, (rows, nb), 1) - off
        tok = _div(lax.broadcasted_iota(jnp.int32, (rows, nb), 0), MLA_HEADS)
        sc = jnp.where((rel >= 0) & (rel <= tok), sc, NEG_INF)
        m = jnp.max(sc, axis=1, keepdims=True)
        e = jnp.exp2(sc - m)
        m_ref[...] = jnp.broadcast_to(m, m_ref.shape)
        l_ref[...] = jnp.broadcast_to(jnp.sum(e, axis=1, keepdims=True), l_ref.shape)
        acc_ref[...] = jnp.dot(e.astype(BF16), kn[:, 0:MLA_KV_LORA], preferred_element_type=F32)

    ckv = jnp.concatenate([r[...] for r in c_refs], axis=0).astype(BF16)
    kpe_t = jnp.concatenate([r[...] for r in p_refs], axis=1).astype(BF16)
    sc = (lax.dot_general(q[:, 0:MLA_KV_LORA], ckv, _NT, preferred_element_type=F32)
          + jnp.dot(q[:, MLA_KV_LORA:MLA_KV_LORA + MLA_ROPE], kpe_t, preferred_element_type=F32))
    _softmax_step(sc, ckv, m_ref, l_ref, acc_ref)

    @pl.when(s == n_steps - 1)
    def _():
        o_ref[...] = (acc_ref[...] / _tile_lanes(l_ref[...], MLA_KV_LORA)).astype(o_ref.dtype)


def _mla_sample(q_s, k_cat, n_prompt, cache_ckv, cache_kpe_t, layer, page_table, t_new):
    db, n_pages = page_table.shape
    pps = math.gcd(MLA_PAGES_PER_STEP, n_pages)
    steps = n_pages // pps
    rows = MLA_HEADS * t_new
    nb = 16
    per = nb // t_new
    new_blk = n_prompt // nb

    def page_spec(rows_, width, r):
        return pl.BlockSpec((None, None, rows_, width), lambda b, s, pt: (layer, pt[b, s * pps + r], 0, 0))

    kern = functools.partial(_mla_sample_kernel, t_new=t_new, pps=pps, n_steps=steps)
    grid_spec = pltpu.PrefetchScalarGridSpec(
        num_scalar_prefetch=1,
        grid=(db, steps),
        in_specs=[
            pl.BlockSpec((rows, MLA_QW), lambda b, s, pt: (b, 0)),
            pl.BlockSpec((nb, MLA_QW), lambda b, s, pt: (new_blk + b // per, 0)),
        ] + [page_spec(PAGE_SIZE, MLA_KV_LORA, r) for r in range(pps)]
          + [page_spec(MLA_ROPE, PAGE_SIZE, r) for r in range(pps)],
        out_specs=pl.BlockSpec((rows, MLA_KV_LORA), lambda b, s, pt: (b, 0)),
        scratch_shapes=[pltpu.VMEM((rows, LANES), F32), pltpu.VMEM((rows, LANES), F32),
                        pltpu.VMEM((rows, MLA_KV_LORA), F32)],
    )
    return pl.pallas_call(
        kern,
        grid_spec=grid_spec,
        out_shape=jax.ShapeDtypeStruct((db * rows, MLA_KV_LORA), BF16),
        compiler_params=_params("parallel", "arbitrary"),
        name="mla_sample_attn",
    )(page_table, q_s, k_cat, *([cache_ckv] * pps), *([cache_kpe_t] * pps))


def _mla_out_kernel(ol_ref, gate_ref, wuv_ref, w_ref, x_ref, g_ref, y_ref):
    tiles = ol_ref.shape[0]
    o = jnp.concatenate(
        [jnp.dot(jnp.concatenate([ol_ref[t, h] for t in range(tiles)], axis=0), wuv_ref[h],
                 preferred_element_type=F32) for h in range(MLA_HEADS)], axis=1)
    a = (o * _silu(gate_ref[...])).astype(BF16)
    y = jnp.dot(a, w_ref[...], preferred_element_type=F32)
    y_ref[...] = x_ref[...] + _rms(y, g_ref[...])


def _mla_out(o_lat, p, wuv, w, x, g):
    n, d = x.shape
    tm = 256
    return pl.pallas_call(
        _mla_out_kernel,
        grid=(n // tm,),
        in_specs=[
            pl.BlockSpec((tm // MLA_TQ, MLA_HEADS, MLA_TQ, MLA_KV_LORA), lambda i: (i, 0, 0, 0)),
            pl.BlockSpec((tm, d), lambda i: (i, 0)),
            pl.BlockSpec(wuv.shape, lambda i: (0, 0, 0)),
            pl.BlockSpec((d, d), lambda i: (0, 0)),
            pl.BlockSpec((tm, d), lambda i: (i, 0)),
            pl.BlockSpec((1, d), lambda i: (0, 0)),
        ],
        out_specs=pl.BlockSpec((tm, d), lambda i: (i, 0)),
        out_shape=jax.ShapeDtypeStruct((n, d), F32),
        compiler_params=_params("parallel"),
        name="mla_out_norm",
    )(o_lat, p, wuv, w, x, g.reshape(1, d))


def _rope_tables(pos):
    half = ROPE_DIM // 2
    inv = jnp.float32(ROPE_THETA) ** (-jnp.arange(half, dtype=F32) * 2.0 / ROPE_DIM)
    ang = pos.astype(F32)[:, None] * inv[None, :]
    cos, sin = jnp.cos(ang), jnp.sin(ang)
    cos = jnp.concatenate([cos, cos, cos, cos], axis=1)
    sin = jnp.concatenate([-sin, sin, -sin, sin], axis=1)
    return cos, sin


def _pad_rows(a, n):
    return a if a.shape[0] == n else jnp.concatenate([a, jnp.zeros((n - a.shape[0],) + a.shape[1:], a.dtype)], axis=0)


def kernel(x_prompt, x_sample, cache_sb_k, cache_sb_v, state_swa_k, state_swa_v, cache_mla_ckv, cache_mla_kpe,
           page_table, norm_pre, norm_post, sb_w_in, sb_w_out, swa_w_in, swa_sinks, swa_w_out,
           mla_w_in, mla_q_norm, mla_w_uq, mla_kv_norm, mla_w_uk, mla_w_uv, mla_w_out):
    batch, seq, d = x_prompt.shape
    db, t_new, _ = x_sample.shape
    n_p, n_s = batch * seq, db * t_new
    n = n_p + n_s
    n_pad = -(-n // ROW_TILE) * ROW_TILE
    past = page_table.shape[1] * PAGE_SIZE
    x = _pad_rows(jnp.concatenate([x_prompt.reshape(n_p, d), x_sample.reshape(n_s, d)], axis=0), n_pad)
    pos = jnp.concatenate([jnp.tile(jnp.arange(seq), batch), jnp.tile(past + jnp.arange(t_new), db),
                           jnp.zeros((n_pad - n,), jnp.int32)])
    cos, sin = _rope_tables(pos)

    sb_st, swa_st, mla_st = [], [], []
    for i in range(DEPTH):
        kind, j = i % N_MIXERS, i // N_MIXERS
        if kind == 0:
            qo, ko, vo, go = 0, SB_WIDTH, SB_WIDTH + SB_KV_WIDTH, SB_WIDTH + 2 * SB_KV_WIDTH
            w_in = sb_w_in[j]
            w = jnp.concatenate([w_in[:, go:], w_in[:, qo:ko], w_in[:, ko:vo], w_in[:, vo:go]], axis=1).astype(BF16)
            p = _proj(x, norm_pre[i], w, cos, sin, tn=512)
            o_p = _sb_prompt(p, batch, seq)
            q_s = p[n_p:n, D_MODEL:D_MODEL + SB_WIDTH].reshape(db, t_new, SB_KV_HEADS, SB_GROUPS, SB_HEAD_DIM)
            q_s = q_s.transpose(0, 2, 3, 1, 4).reshape(db, SB_KV_HEADS, SB_GROUPS * t_new, SB_HEAD_DIM)
            o_s = _sb_sample(q_s, p, n_p, cache_sb_k, cache_sb_v, j, page_table)
            o_s = o_s.reshape(db, SB_KV_HEADS, SB_GROUPS, t_new, SB_HEAD_DIM).transpose(0, 3, 1, 2, 4).reshape(n_s, SB_WIDTH)
            o = _pad_rows(jnp.concatenate([o_p, o_s], axis=0), n_pad)
            x = _out(o, p, sb_w_out[j].astype(BF16), x, norm_post[i])
            k_all = p[:, D_MODEL + SB_WIDTH:D_MODEL + SB_WIDTH + SB_KV_WIDTH]
            v_all = p[:, D_MODEL + SB_WIDTH + SB_KV_WIDTH:]
            sb_st.append((k_all[:n_p].reshape(batch, seq, SB_KV_HEADS, SB_HEAD_DIM),
                          v_all[:n_p].reshape(batch, seq, SB_KV_HEADS, SB_HEAD_DIM),
                          k_all[n_p:n].reshape(db, t_new, SB_KV_HEADS, SB_HEAD_DIM),
                          v_all[n_p:n].reshape(db, t_new, SB_KV_HEADS, SB_HEAD_DIM)))
        elif kind == 1:
            qo, ko, vo, go = 0, SWA_WIDTH, SWA_WIDTH + SWA_KV_WIDTH, SWA_WIDTH + 2 * SWA_KV_WIDTH
            w_in = swa_w_in[j]
            w = jnp.concatenate([w_in[:, go:], w_in[:, qo:ko], w_in[:, ko:vo], w_in[:, vo:go]], axis=1).astype(BF16)
            p = _proj(x, norm_pre[i], w, cos, sin, tn=512, rope_cols=(D_MODEL, D_MODEL + SWA_WIDTH + SWA_KV_WIDTH))
            sinks = swa_sinks[j].astype(F32)
            o_p = _swa_prompt(p, sinks, batch, seq)
            w_buf = state_swa_k.shape[2]
            st_k = state_swa_k[j].reshape(db, w_buf, SWA_KV_WIDTH)
            st_v = state_swa_v[j].reshape(db, w_buf, SWA_KV_WIDTH)
            o_s = _swa_sample(p, sinks, n_p, st_k, st_v, t_new)
            o = _pad_rows(jnp.concatenate([o_p, o_s], axis=0), n_pad)
            x = _out(o, p, swa_w_out[j].astype(BF16), x, norm_post[i])
            k_all = p[:, D_MODEL + SWA_WIDTH:D_MODEL + SWA_WIDTH + SWA_KV_WIDTH]
            v_all = p[:, D_MODEL + SWA_WIDTH + SWA_KV_WIDTH:]
            keep = min(WINDOW, seq)
            kp = k_all[:n_p].reshape(batch, seq, SWA_KV_HEADS, SWA_HEAD_DIM)[:, seq - keep:]
            vp = v_all[:n_p].reshape(batch, seq, SWA_KV_HEADS, SWA_HEAD_DIM)[:, seq - keep:]
            ks = jnp.concatenate([st_k, k_all[n_p:n].reshape(db, t_new, SWA_KV_WIDTH)], axis=1)[:, t_new:]
            vs = jnp.concatenate([st_v, v_all[n_p:n].reshape(db, t_new, SWA_KV_WIDTH)], axis=1)[:, t_new:]
            swa_st.append((kp, vp, ks.reshape(db, w_buf, SWA_KV_HEADS, SWA_HEAD_DIM),
                           vs.reshape(db, w_buf, SWA_KV_HEADS, SWA_HEAD_DIM)))
        else:
            w_in = mla_w_in[j]
            o_kv = MLA_Q_LORA + MLA_KV_LORA + MLA_ROPE
            pad = jnp.zeros((d, 1024 - o_kv), w_in.dtype)
            w = jnp.concatenate([w_in[:, o_kv:], w_in[:, :o_kv], pad], axis=1).astype(BF16)
            p = _proj(x, norm_pre[i], w, cos, sin, tn=512)
            wuq = mla_w_uq[j].reshape(MLA_Q_LORA, MLA_HEADS, MLA_QK)
            wuq = jnp.concatenate([wuq[:, :, :MLA_NOPE].reshape(MLA_Q_LORA, -1),
                                   wuq[:, :, MLA_NOPE:].reshape(MLA_Q_LORA, -1)], axis=1).astype(BF16)
            wuk = mla_w_uk[j].transpose(1, 2, 0).astype(BF16)
            wuv = mla_w_uv[j].transpose(1, 0, 2).astype(BF16)
            q_cat, ckv, kpe, k_cat = _mla_prep(p, mla_q_norm[j], wuq, mla_kv_norm[j], wuk, cos, sin)
            o_p = _mla_prompt(q_cat, k_cat, batch, seq)
            tiles_p = n_p // MLA_TQ
            q_s = q_cat[tiles_p:].transpose(0, 2, 1, 3).reshape(-1, MLA_QW)
            o_s = _mla_sample(q_s, k_cat, n_p, cache_mla_ckv, cache_mla_kpe.transpose(0, 1, 3, 2), j, page_table, t_new)
            o_s = _pad_rows(o_s, (n_pad - n_p) * MLA_HEADS).reshape(-1, MLA_TQ, MLA_HEADS, MLA_KV_LORA).transpose(0, 2, 1, 3)
            x = _mla_out(jnp.concatenate([o_p, o_s], axis=0), p, wuv, mla_w_out[j].astype(BF16), x, norm_post[i])
            mla_st.append((ckv[:n_p].reshape(batch, seq, MLA_KV_LORA), kpe[:n_p].reshape(batch, seq, MLA_ROPE),
                           ckv[n_p:n].reshape(db, t_new, MLA_KV_LORA), kpe[n_p:n].reshape(db, t_new, MLA_ROPE)))

    def stack(st, k):
        return jnp.stack([s[k] for s in st])

    return (x[:n_p].reshape(batch, seq, d), x[n_p:n].reshape(db, t_new, d),
            stack(sb_st, 0), stack(sb_st, 1), stack(sb_st, 2), stack(sb_st, 3),
            stack(swa_st, 0), stack(swa_st, 1), stack(swa_st, 2), stack(swa_st, 3),
            stack(mla_st, 0), stack(mla_st, 1), stack(mla_st, 2), stack(mla_st, 3))
```

```python
import functools
import math

import jax
import jax.numpy as jnp
from jax import lax
from jax.experimental import pallas as pl
from jax.experimental.pallas import tpu as pltpu

F32 = jnp.float32
BF16 = jnp.bfloat16

D_MODEL = 2048
DEPTH = 4
N_MIXERS = 3
PAGE_SIZE = 128
ROPE_THETA = 10000.0
NORM_EPS = 1e-6
NEG_INF = -1e30
LOG2E = math.log2(math.e)

SB_HEADS = 16
SB_KV_HEADS = 4
SB_HEAD_DIM = D_MODEL // SB_HEADS
SB_GROUPS = SB_HEADS // SB_KV_HEADS
SB_WIDTH = SB_HEADS * SB_HEAD_DIM
SB_KV_WIDTH = SB_KV_HEADS * SB_HEAD_DIM

SWA_HEADS = 32
SWA_KV_HEADS = 8
SWA_HEAD_DIM = D_MODEL // SWA_HEADS
SWA_GROUPS = SWA_HEADS // SWA_KV_HEADS
SWA_WIDTH = SWA_HEADS * SWA_HEAD_DIM
SWA_KV_WIDTH = SWA_KV_HEADS * SWA_HEAD_DIM
WINDOW = 128

MLA_HEADS = 16
MLA_Q_LORA = D_MODEL // 4
MLA_KV_LORA = D_MODEL // 8
MLA_NOPE = 128
MLA_ROPE = 64
MLA_QK = MLA_NOPE + MLA_ROPE
MLA_V = 128
MLA_WIDTH = MLA_HEADS * MLA_V

LANES = 128
ROPE_DIM = 64
VMEM_LIMIT = 56 * 1024 * 1024

ROW_TILE = 512
PAGES_PER_STEP = 16
MLA_PAGES_PER_STEP = 32
MLA_TQ = 64
MLA_QW = MLA_KV_LORA + LANES

_NT = (((1,), (1,)), ((), ()))


def _params(*sem):
    return pltpu.CompilerParams(dimension_semantics=sem, vmem_limit_bytes=VMEM_LIMIT)


def _div(x, n):
    assert n & (n - 1) == 0
    return x >> (n.bit_length() - 1)


def _mod(x, n):
    assert n & (n - 1) == 0
    return x & (n - 1)


def _tile_lanes(x, width):
    reps = width // LANES
    return x if reps == 1 else jnp.concatenate([x] * reps, axis=1)


def _silu(x):
    return x / (1.0 + jnp.exp(-x))


def _rms(x, g):
    return x * lax.rsqrt(jnp.mean(x * x, axis=-1, keepdims=True) + NORM_EPS) * g


def _rope_lanes(x, cos, sin_signed):
    w = x.shape[1]
    cos, sin_signed = _tile_lanes(cos, w), _tile_lanes(sin_signed, w)
    lane = lax.broadcasted_iota(jnp.int32, x.shape, 1)
    first_half = (lane & (ROPE_DIM - 1)) < (ROPE_DIM // 2)
    partner = jnp.where(first_half, pltpu.roll(x, w - ROPE_DIM // 2, 1), pltpu.roll(x, ROPE_DIM // 2, 1))
    return x * cos + partner * sin_signed


def _proj_kernel(x_ref, g_ref, w_ref, cos_ref, sin_ref, o_ref, h_ref, *, rope_lo, rope_hi):
    j = pl.program_id(1)

    @pl.when(j == 0)
    def _():
        h_ref[...] = _rms(x_ref[...], g_ref[...]).astype(BF16)

    acc = jnp.dot(h_ref[...], w_ref[...], preferred_element_type=F32)
    if rope_hi > rope_lo:
        in_rope = (j >= rope_lo) & (j < rope_hi)

        @pl.when(in_rope)
        def _():
            o_ref[...] = _rope_lanes(acc, cos_ref[...], sin_ref[...])

        @pl.when(jnp.logical_not(in_rope))
        def _():
            o_ref[...] = acc
    else:
        o_ref[...] = acc


def _proj(x, g, w, cos, sin, *, tn, rope_cols=(0, 0)):
    n, d = x.shape
    c = w.shape[1]
    tm = ROW_TILE
    kern = functools.partial(_proj_kernel, rope_lo=rope_cols[0] // tn, rope_hi=rope_cols[1] // tn)
    return pl.pallas_call(
        kern,
        grid=(n // tm, c // tn),
        in_specs=[
            pl.BlockSpec((tm, d), lambda i, j: (i, 0)),
            pl.BlockSpec((1, d), lambda i, j: (0, 0)),
            pl.BlockSpec((d, tn), lambda i, j: (0, j)),
            pl.BlockSpec((tm, LANES), lambda i, j: (i, 0)),
            pl.BlockSpec((tm, LANES), lambda i, j: (i, 0)),
        ],
        out_specs=pl.BlockSpec((tm, tn), lambda i, j: (i, j)),
        out_shape=jax.ShapeDtypeStruct((n, c), F32),
        scratch_shapes=[pltpu.VMEM((tm, d), BF16)],
        compiler_params=_params("parallel", "arbitrary"),
        name="norm_proj",
    )(x, g.reshape(1, d), w, cos, sin)


def _out_kernel(op_ref, os_ref, gate_ref, w_ref, x_ref, g_ref, y_ref, *, prompt_blocks):
    o = jnp.where(pl.program_id(0) < prompt_blocks, op_ref[...], os_ref[...])
    a = (o * _silu(gate_ref[...])).astype(BF16)
    y = jnp.dot(a, w_ref[...], preferred_element_type=F32)
    y_ref[...] = x_ref[...] + _rms(y, g_ref[...])


def _out(o_p, o_s, p, w, x, g):
    n, d = x.shape
    tm = 256
    npb = o_p.shape[0] // tm
    kern = functools.partial(_out_kernel, prompt_blocks=npb)
    return pl.pallas_call(
        kern,
        grid=(n // tm,),
        in_specs=[
            pl.BlockSpec((tm, d), lambda i: (jnp.minimum(i, npb - 1), 0)),
            pl.BlockSpec((tm, d), lambda i: (jnp.maximum(i - npb, 0), 0)),
            pl.BlockSpec((tm, d), lambda i: (i, 0)),
            pl.BlockSpec((d, d), lambda i: (0, 0)),
            pl.BlockSpec((tm, d), lambda i: (i, 0)),
            pl.BlockSpec((1, d), lambda i: (0, 0)),
        ],
        out_specs=pl.BlockSpec((tm, d), lambda i: (i, 0)),
        out_shape=jax.ShapeDtypeStruct((n, d), F32),
        compiler_params=_params("parallel"),
        name="gate_out_norm",
    )(o_p, o_s, p, w, x, g.reshape(1, d))


def _suffix_ones():
    s = lax.broadcasted_iota(jnp.int32, (LANES, LANES), 0)
    j = lax.broadcasted_iota(jnp.int32, (LANES, LANES), 1)
    u = (s > j).astype(BF16)
    return jnp.concatenate([u, u], axis=0)


def _sb_weights(t, carry, u2, valid):
    rows, width = t.shape
    nb = width // LANES
    drop = jnp.maximum(t, 0.0) + jnp.log(1.0 + jnp.exp2(-jnp.abs(t))) * LOG2E
    if valid is not None:
        drop = jnp.where(valid, drop, 0.0)
    hi = drop.astype(BF16)
    lo = (drop - hi.astype(F32)).astype(BF16)
    blocks = [jnp.concatenate([hi[:, p * LANES:(p + 1) * LANES], lo[:, p * LANES:(p + 1) * LANES]], axis=1)
              for p in range(nb)]
    later = jnp.dot(jnp.concatenate(blocks, axis=0), u2, preferred_element_type=F32)
    cols = [None] * nb
    for p in reversed(range(nb)):
        cols[p] = later[p * rows:(p + 1) * rows] + carry
        carry = carry + jnp.sum(drop[:, p * LANES:(p + 1) * LANES], axis=1, keepdims=True)
    w = jnp.exp2(t - drop - jnp.concatenate(cols, axis=1))
    if valid is not None:
        w = jnp.where(valid, w, 0.0)
    return w, carry


def _sb_prompt_kernel(q_ref, k_ref, v_ref, u_ref, o_ref, acc_ref, carry_ref, *, tq, tk):
    i = pl.program_id(2)
    groups = SB_GROUPS
    rows = groups * tq
    c = (SB_HEAD_DIM ** -0.5) * LOG2E
    q = jnp.concatenate([q_ref[:, g * SB_HEAD_DIM:(g + 1) * SB_HEAD_DIM] for g in range(groups)], axis=0)
    q = (q * c).astype(BF16)
    u2 = u_ref[...]
    diag = (i * tq) // tk

    def block(kb, valid):
        start = pl.multiple_of(kb * tk, tk)
        k = k_ref[pl.ds(start, tk), :].astype(BF16)
        v = v_ref[pl.ds(start, tk), :].astype(BF16)
        t = lax.dot_general(q, k, _NT, preferred_element_type=F32)
        w, carry = _sb_weights(t, carry_ref[...], u2, valid)
        carry_ref[...] = carry
        acc_ref[...] += jnp.dot(w.astype(BF16), v, preferred_element_type=F32)

    acc_ref[...] = jnp.zeros_like(acc_ref)
    carry_ref[...] = jnp.zeros_like(carry_ref)
    q_pos = i * tq + _mod(lax.broadcasted_iota(jnp.int32, (rows, tk), 0), tq)
    k_pos = diag * tk + lax.broadcasted_iota(jnp.int32, (rows, tk), 1)
    block(diag, k_pos < q_pos)

    def body(s, carry):
        block(diag - 1 - s, None)
        return carry

    lax.fori_loop(0, diag, body, 0)
    for g in range(groups):
        o_ref[:, g * SB_HEAD_DIM:(g + 1) * SB_HEAD_DIM] = acc_ref[g * tq:(g + 1) * tq, :]


def _sb_prompt(p, batch, seq):
    tq, tk = 512, 512
    nq = seq // tq
    qw = SB_GROUPS * SB_HEAD_DIM
    q_col, k_col, v_col = D_MODEL // qw, (D_MODEL + SB_WIDTH) // SB_HEAD_DIM, (D_MODEL + SB_WIDTH + SB_KV_WIDTH) // SB_HEAD_DIM
    kern = functools.partial(_sb_prompt_kernel, tq=tq, tk=tk)
    return pl.pallas_call(
        kern,
        grid=(batch, SB_KV_HEADS, nq),
        in_specs=[
            pl.BlockSpec((tq, qw), lambda b, h, i: (b * nq + i, q_col + h)),
            pl.BlockSpec((seq, SB_HEAD_DIM), lambda b, h, i: (b, k_col + h)),
            pl.BlockSpec((seq, SB_HEAD_DIM), lambda b, h, i: (b, v_col + h)),
            pl.BlockSpec((2 * LANES, LANES), lambda b, h, i: (0, 0)),
        ],
        out_specs=pl.BlockSpec((tq, qw), lambda b, h, i: (b * nq + i, h)),
        out_shape=jax.ShapeDtypeStruct((batch * seq, SB_WIDTH), F32),
        scratch_shapes=[pltpu.VMEM((SB_GROUPS * tq, SB_HEAD_DIM), F32), pltpu.VMEM((SB_GROUPS * tq, LANES), F32)],
        compiler_params=_params("parallel", "parallel", "parallel"),
        name="sb_prompt_attn",
    )(p, p, p, _suffix_ones())


def _page_ring(pt_ref, page_of_step, srcs, bufs, sem, *, layer, pps, n_steps, n_seq):
    b, s = pl.program_id(0), pl.program_id(1)
    g = b * n_steps + s
    slot = lax.rem(g, 2)

    def copies(bb, ss, sl):
        out = []
        for r in range(pps):
            page = pt_ref[bb, page_of_step(ss) + r]
            for k, (src, buf) in enumerate(zip(srcs, bufs)):
                out.append(pltpu.make_async_copy(src.at[layer, page], buf.at[sl, r], sem.at[sl, k]))
        return out

    @pl.when(g == 0)
    def _():
        for c in copies(0, 0, 0):
            c.start()

    @pl.when(g + 1 < n_seq * n_steps)
    def _():
        wrap = s == n_steps - 1
        for c in copies(jnp.where(wrap, b + 1, b), jnp.where(wrap, 0, s + 1), 1 - slot):
            c.start()

    for c in copies(b, s, slot):
        c.wait()
    return slot


def _sb_sample_kernel(pt_ref, q_ref, kn_ref, vn_ref, ck_ref, cv_ref, u_ref, o_ref,
                      acc_ref, carry_ref, kpad_ref, vpad_ref, kbuf, vbuf, sem, *, layer, t_new, pps, n_steps, n_seq):
    b, s = pl.program_id(0), pl.program_id(1)
    slot = _page_ring(pt_ref, lambda ss: (n_steps - 1 - ss) * pps, (ck_ref, cv_ref), (kbuf, vbuf), sem,
                      layer=layer, pps=pps, n_steps=n_steps, n_seq=n_seq)
    kvh, hd = SB_KV_HEADS, SB_HEAD_DIM
    rq = SB_GROUPS * t_new
    c = (hd ** -0.5) * LOG2E
    u2 = u_ref[...]
    qs = [(q_ref[h] * c).astype(BF16) for h in range(kvh)]

    def attend(k_heads, v_heads, valid):
        t = jnp.concatenate([lax.dot_general(qs[h], k_heads[h], _NT, preferred_element_type=F32)
                             for h in range(kvh)], axis=0)
        w, carry = _sb_weights(t, carry_ref[...], u2, valid)
        carry_ref[...] = carry
        w = w.astype(BF16)
        for h in range(kvh):
            acc_ref[h] += jnp.dot(w[h * rq:(h + 1) * rq], v_heads[h], preferred_element_type=F32)

    @pl.when(s == 0)
    def _():
        acc_ref[...] = jnp.zeros_like(acc_ref)
        carry_ref[...] = jnp.zeros_like(carry_ref)
        kpad_ref[...] = jnp.zeros_like(kpad_ref)
        vpad_ref[...] = jnp.zeros_like(vpad_ref)
        nb = kn_ref.shape[0]
        kpad_ref[0:nb, :] = kn_ref[...]
        vpad_ref[0:nb, :] = vn_ref[...]
        off = (b % (nb // t_new)) * t_new
        rel = lax.broadcasted_iota(jnp.int32, (kvh * rq, PAGE_SIZE), 1) - off
        tok = _mod(lax.broadcasted_iota(jnp.int32, (kvh * rq, PAGE_SIZE), 0), t_new)
        attend([kpad_ref[:, h * hd:(h + 1) * hd].astype(BF16) for h in range(kvh)],
               [vpad_ref[:, h * hd:(h + 1) * hd].astype(BF16) for h in range(kvh)],
               (rel >= 0) & (rel < tok))

    def head_rows(buf, h):
        return jnp.concatenate([buf[slot, r, pl.ds(h, PAGE_SIZE, stride=kvh), :] for r in range(pps)],
                               axis=0).astype(BF16)

    attend([head_rows(kbuf, h) for h in range(kvh)], [head_rows(vbuf, h) for h in range(kvh)], None)

    @pl.when(s == n_steps - 1)
    def _():
        o_ref[...] = acc_ref[...]


def _sb_sample(q_s, p, n_prompt, cache_k, cache_v, layer, page_table):
    db, kvh, rq, hd = q_s.shape
    t_new = rq // SB_GROUPS
    n_pages = page_table.shape[1]
    pps = math.gcd(PAGES_PER_STEP, n_pages)
    steps = n_pages // pps
    nb = 8
    kw = SB_KV_WIDTH
    k_col, v_col = (D_MODEL + SB_WIDTH) // kw, (D_MODEL + SB_WIDTH + kw) // kw
    new_blk = n_prompt // nb
    per = nb // t_new
    ck = cache_k.reshape(cache_k.shape[0], cache_k.shape[1], PAGE_SIZE * kvh, hd)
    cv = cache_v.reshape(cache_v.shape[0], cache_v.shape[1], PAGE_SIZE * kvh, hd)

    kern = functools.partial(_sb_sample_kernel, layer=layer, t_new=t_new, pps=pps, n_steps=steps, n_seq=db)
    grid_spec = pltpu.PrefetchScalarGridSpec(
        num_scalar_prefetch=1,
        grid=(db, steps),
        in_specs=[
            pl.BlockSpec((None, kvh, rq, hd), lambda b, s, pt: (b, 0, 0, 0)),
            pl.BlockSpec((nb, kw), lambda b, s, pt: (new_blk + b // per, k_col)),
            pl.BlockSpec((nb, kw), lambda b, s, pt: (new_blk + b // per, v_col)),
            pl.BlockSpec(memory_space=pl.ANY),
            pl.BlockSpec(memory_space=pl.ANY),
            pl.BlockSpec((2 * LANES, LANES), lambda b, s, pt: (0, 0)),
        ],
        out_specs=pl.BlockSpec((None, kvh, rq, hd), lambda b, s, pt: (b, 0, 0, 0)),
        scratch_shapes=[
            pltpu.VMEM((kvh, rq, hd), F32),
            pltpu.VMEM((kvh * rq, LANES), F32),
            pltpu.VMEM((PAGE_SIZE, kw), F32),
            pltpu.VMEM((PAGE_SIZE, kw), F32),
            pltpu.VMEM((2, pps, PAGE_SIZE * kvh, hd), F32),
            pltpu.VMEM((2, pps, PAGE_SIZE * kvh, hd), F32),
            pltpu.SemaphoreType.DMA((2, 2)),
        ],
    )
    return pl.pallas_call(
        kern,
        grid_spec=grid_spec,
        out_shape=jax.ShapeDtypeStruct((db, kvh, rq, hd), F32),
        compiler_params=_params("arbitrary", "arbitrary"),
        name="sb_sample_attn",
    )(page_table, q_s, p, p, ck, cv, _suffix_ones())


def _lane_halves(x):
    lane = lax.broadcasted_iota(jnp.int32, x.shape, 1)
    lo = lane < ROPE_DIM
    z = jnp.zeros_like(x)
    a0 = jnp.where(lo, x, z)
    b1 = jnp.where(lo, z, x)
    return a0, pltpu.roll(a0, ROPE_DIM, 1), pltpu.roll(b1, ROPE_DIM, 1), b1


def _swa_heads(q, k, v, valid, sinks_ref, sink_base):
    ka0, ka1, kb0, kb1 = _lane_halves(k.astype(BF16))
    va0, va1, vb0, vb1 = _lane_halves(v.astype(BF16))
    outs = []
    for m in range(4):
        qm = q[:, m * LANES:(m + 1) * LANES].astype(BF16)
        k0, k1, v0, v1 = (ka0, ka1, va0, va1) if m < 2 else (kb0, kb1, vb0, vb1)
        o = None
        for half, (kk, vv) in enumerate(((k0, v0), (k1, v1))):
            s = lax.dot_general(qm, kk, _NT, preferred_element_type=F32) * (SWA_HEAD_DIM ** -0.5)
            s = jnp.where(valid, s, NEG_INF)
            sink = sinks_ref[sink_base + 2 * m + half]
            mx = jnp.maximum(jnp.max(s, axis=-1, keepdims=True), sink)
            e = jnp.exp(s - mx)
            inv = 1.0 / (jnp.sum(e, axis=-1, keepdims=True) + jnp.exp(sink - mx))
            part = jnp.dot((e * inv).astype(BF16), vv, preferred_element_type=F32)
            o = part if o is None else o + part
        outs.append(o)
    return jnp.concatenate(outs, axis=1)


def _swa_prompt_kernel(sinks_ref, q_ref, kp_ref, kc_ref, vp_ref, vc_ref, o_ref):
    i, c = pl.program_id(1), pl.program_id(2)
    tq = q_ref.shape[0]
    k = jnp.concatenate([kp_ref[...], kc_ref[...]], axis=0)
    v = jnp.concatenate([vp_ref[...], vc_ref[...]], axis=0)
    qi = lax.broadcasted_iota(jnp.int32, (tq, 2 * tq), 0)
    kj = lax.broadcasted_iota(jnp.int32, (tq, 2 * tq), 1) - tq
    diff = qi - kj
    valid = (diff >= 0) & (diff < WINDOW) & (kj + i * tq >= 0)
    o_ref[...] = _swa_heads(q_ref[...], k, v, valid, sinks_ref, 8 * c)


def _swa_prompt(p, sinks, batch, seq):
    tq = WINDOW
    nq = seq // tq
    qw = 8 * SWA_HEAD_DIM
    q_col = D_MODEL // qw
    k_col = (D_MODEL + SWA_WIDTH) // LANES
    v_col = (D_MODEL + SWA_WIDTH + SWA_KV_WIDTH) // LANES
    grid_spec = pltpu.PrefetchScalarGridSpec(
        num_scalar_prefetch=1,
        grid=(batch, nq, SWA_KV_HEADS // 2),
        in_specs=[
            pl.BlockSpec((tq, qw), lambda b, i, c, sk: (b * nq + i, q_col + c)),
            pl.BlockSpec((tq, LANES), lambda b, i, c, sk: (b * nq + jnp.maximum(i - 1, 0), k_col + c)),
            pl.BlockSpec((tq, LANES), lambda b, i, c, sk: (b * nq + i, k_col + c)),
            pl.BlockSpec((tq, LANES), lambda b, i, c, sk: (b * nq + jnp.maximum(i - 1, 0), v_col + c)),
            pl.BlockSpec((tq, LANES), lambda b, i, c, sk: (b * nq + i, v_col + c)),
        ],
        out_specs=pl.BlockSpec((tq, qw), lambda b, i, c, sk: (b * nq + i, c)),
    )
    return pl.pallas_call(
        _swa_prompt_kernel,
        grid_spec=grid_spec,
        out_shape=jax.ShapeDtypeStruct((batch * seq, SWA_WIDTH), F32),
        compiler_params=_params("parallel", "parallel", "parallel"),
        name="swa_prompt_attn",
    )(sinks, p, p, p, p, p)


def _swa_sample_kernel(sinks_ref, q_ref, kn_ref, vn_ref, ks_ref, vs_ref, o_ref, *, t_new):
    c = pl.program_id(1)
    nb = q_ref.shape[0]
    per = nb // t_new
    w_buf = ks_ref.shape[1]
    k = jnp.concatenate([ks_ref[s] for s in range(per)] + [kn_ref[...]], axis=0)
    v = jnp.concatenate([vs_ref[s] for s in range(per)] + [vn_ref[...]], axis=0)
    nk = per * w_buf + nb
    row = lax.broadcasted_iota(jnp.int32, (nb, nk), 0)
    col = lax.broadcasted_iota(jnp.int32, (nb, nk), 1)
    q_seq, q_tok = _div(row, t_new), _mod(row, t_new)
    in_state = col < per * w_buf
    new_col = jnp.maximum(col - per * w_buf, 0)
    k_seq = jnp.where(in_state, _div(col, w_buf), _div(new_col, t_new))
    k_idx = jnp.where(in_state, _mod(col, w_buf), w_buf + _mod(new_col, t_new))
    diff = w_buf + q_tok - k_idx
    valid = (k_seq == q_seq) & (diff >= 0) & (diff < WINDOW)
    o_ref[...] = _swa_heads(q_ref[...], k, v, valid, sinks_ref, 8 * c)


def _swa_sample(p, sinks, n_prompt, state_k, state_v, t_new):
    db, w_buf, kw = state_k.shape
    nb = 32 if (db * t_new) % 32 == 0 else 8
    per = nb // t_new
    qw = 8 * SWA_HEAD_DIM
    q_col = D_MODEL // qw
    k_col = (D_MODEL + SWA_WIDTH) // LANES
    v_col = (D_MODEL + SWA_WIDTH + SWA_KV_WIDTH) // LANES
    new_blk = n_prompt // nb
    kern = functools.partial(_swa_sample_kernel, t_new=t_new)
    grid_spec = pltpu.PrefetchScalarGridSpec(
        num_scalar_prefetch=1,
        grid=(db // per, SWA_KV_HEADS // 2),
        in_specs=[
            pl.BlockSpec((nb, qw), lambda g, c, sk: (new_blk + g, q_col + c)),
            pl.BlockSpec((nb, LANES), lambda g, c, sk: (new_blk + g, k_col + c)),
            pl.BlockSpec((nb, LANES), lambda g, c, sk: (new_blk + g, v_col + c)),
            pl.BlockSpec((per, w_buf, LANES), lambda g, c, sk: (g, 0, c)),
            pl.BlockSpec((per, w_buf, LANES), lambda g, c, sk: (g, 0, c)),
        ],
        out_specs=pl.BlockSpec((nb, qw), lambda g, c, sk: (g, c)),
    )
    return pl.pallas_call(
        kern,
        grid_spec=grid_spec,
        out_shape=jax.ShapeDtypeStruct((db * t_new, SWA_WIDTH), F32),
        compiler_params=_params("parallel", "parallel"),
        name="swa_sample_attn",
    )(sinks, p, p, p, state_k, state_v)


def _mla_prep_kernel(p_ref, qn_ref, wuq_ref, kvn_ref, wuk_ref, cos_ref, sin_ref,
                     qcat_ref, ckv_ref, kpe_ref, kcat_ref):
    tm = p_ref.shape[0]
    c = (MLA_QK ** -0.5) * LOG2E
    c_q = _rms(p_ref[:, 0:MLA_Q_LORA], qn_ref[...]).astype(BF16)
    c_kv = _rms(p_ref[:, MLA_Q_LORA:MLA_Q_LORA + MLA_KV_LORA], kvn_ref[...])
    cos, sin = cos_ref[...], sin_ref[...]
    o = MLA_Q_LORA + MLA_KV_LORA
    k_pe = _rope_lanes(p_ref[:, o:o + LANES], cos, sin)
    ckv_ref[...] = c_kv
    kpe_ref[...] = k_pe[:, 0:MLA_ROPE]
    kcat_ref[:, 0:MLA_KV_LORA] = c_kv.astype(BF16)
    kcat_ref[:, MLA_KV_LORA:] = k_pe.astype(BF16)
    q = jnp.dot(c_q, wuq_ref[...], preferred_element_type=F32)
    q_pe = _rope_lanes(q[:, MLA_HEADS * MLA_NOPE:], cos, sin) * c
    lane = lax.broadcasted_iota(jnp.int32, (tm, LANES), 1)
    for h in range(MLA_HEADS):
        qn = q[:, h * MLA_NOPE:(h + 1) * MLA_NOPE].astype(BF16)
        q_lat = (jnp.dot(qn, wuk_ref[h], preferred_element_type=F32) * c).astype(BF16)
        x = q_pe[:, (h // 2) * LANES:(h // 2 + 1) * LANES]
        if h % 2:
            x = pltpu.roll(x, MLA_ROPE, 1)
        pe = jnp.where(lane < MLA_ROPE, x, 0.0).astype(BF16)
        for tile in range(tm // MLA_TQ):
            r = slice(tile * MLA_TQ, (tile + 1) * MLA_TQ)
            qcat_ref[tile, h, :, 0:MLA_KV_LORA] = q_lat[r]
            qcat_ref[tile, h, :, MLA_KV_LORA:] = pe[r]


def _mla_prep(p, q_norm, wuq, kv_norm, wuk, cos, sin):
    n = p.shape[0]
    tm = 256
    pw = p.shape[1] - D_MODEL
    return pl.pallas_call(
        _mla_prep_kernel,
        grid=(n // tm,),
        in_specs=[
            pl.BlockSpec((tm, pw), lambda i: (i, D_MODEL // pw)),
            pl.BlockSpec((1, MLA_Q_LORA), lambda i: (0, 0)),
            pl.BlockSpec(wuq.shape, lambda i: (0, 0)),
            pl.BlockSpec((1, MLA_KV_LORA), lambda i: (0, 0)),
            pl.BlockSpec(wuk.shape, lambda i: (0, 0, 0)),
            pl.BlockSpec((tm, LANES), lambda i: (i, 0)),
            pl.BlockSpec((tm, LANES), lambda i: (i, 0)),
        ],
        out_specs=[
            pl.BlockSpec((tm // MLA_TQ, MLA_HEADS, MLA_TQ, MLA_QW), lambda i: (i, 0, 0, 0)),
            pl.BlockSpec((tm, MLA_KV_LORA), lambda i: (i, 0)),
            pl.BlockSpec((tm, MLA_ROPE), lambda i: (i, 0)),
            pl.BlockSpec((tm, MLA_QW), lambda i: (i, 0)),
        ],
        out_shape=[
            jax.ShapeDtypeStruct((n // MLA_TQ, MLA_HEADS, MLA_TQ, MLA_QW), BF16),
            jax.ShapeDtypeStruct((n, MLA_KV_LORA), F32),
            jax.ShapeDtypeStruct((n, MLA_ROPE), F32),
            jax.ShapeDtypeStruct((n, MLA_QW), BF16),
        ],
        compiler_params=_params("parallel"),
        name="mla_prep",
    )(p, q_norm.reshape(1, -1), wuq, kv_norm.reshape(1, -1), wuk, cos, sin)


def _softmax_step(s, v, m_ref, l_ref, acc_ref):
    m_old = m_ref[...]
    m_new = jnp.maximum(m_old, jnp.max(s, axis=1, keepdims=True))
    alpha = jnp.exp2(m_old - m_new)
    e = jnp.exp2(s - _tile_lanes(m_new, s.shape[1]))
    l_ref[...] = alpha * l_ref[...] + jnp.sum(e, axis=1, keepdims=True)
    acc_ref[...] = _tile_lanes(alpha, acc_ref.shape[1]) * acc_ref[...] + jnp.dot(
        e.astype(BF16), v, preferred_element_type=F32)
    m_ref[...] = m_new


def _mla_prompt_kernel(q_ref, k_ref, o_ref, m_ref, l_ref, acc_ref, *, tk):
    i = pl.program_id(1)
    tq = MLA_TQ
    rows = MLA_HEADS * tq
    q = jnp.concatenate([q_ref[h] for h in range(MLA_HEADS)], axis=0)
    m_ref[...] = jnp.full_like(m_ref, NEG_INF)
    l_ref[...] = jnp.zeros_like(l_ref)
    acc_ref[...] = jnp.zeros_like(acc_ref)
    diag = (i * tq) // tk

    def block(kb, masked):
        start = pl.multiple_of(kb * tk, tk)
        kc = k_ref[pl.ds(start, tk), :]
        s = lax.dot_general(q, kc, _NT, preferred_element_type=F32)
        if masked:
            q_pos = i * tq + _mod(lax.broadcasted_iota(jnp.int32, (rows, tk), 0), tq)
            k_pos = kb * tk + lax.broadcasted_iota(jnp.int32, (rows, tk), 1)
            s = jnp.where(k_pos <= q_pos, s, NEG_INF)
        _softmax_step(s, kc[:, 0:MLA_KV_LORA], m_ref, l_ref, acc_ref)

    def body(kb, carry):
        block(kb, False)
        return carry

    lax.fori_loop(0, diag, body, 0)
    block(diag, True)
    o = (acc_ref[...] / _tile_lanes(l_ref[...], MLA_KV_LORA)).astype(o_ref.dtype)
    for h in range(MLA_HEADS):
        o_ref[h] = o[h * tq:(h + 1) * tq]


def _mla_prompt(q_cat, k_cat, batch, seq):
    tk = 512
    nq = seq // MLA_TQ
    rows = MLA_HEADS * MLA_TQ
    kern = functools.partial(_mla_prompt_kernel, tk=tk)
    return pl.pallas_call(
        kern,
        grid=(batch, nq),
        in_specs=[
            pl.BlockSpec((None, MLA_HEADS, MLA_TQ, MLA_QW), lambda b, i: (b * nq + i, 0, 0, 0)),
            pl.BlockSpec((seq, MLA_QW), lambda b, i: (b, 0)),
        ],
        out_specs=pl.BlockSpec((None, MLA_HEADS, MLA_TQ, MLA_KV_LORA), lambda b, i: (b * nq + i, 0, 0, 0)),
        out_shape=jax.ShapeDtypeStruct((batch * nq, MLA_HEADS, MLA_TQ, MLA_KV_LORA), BF16),
        scratch_shapes=[pltpu.VMEM((rows, LANES), F32), pltpu.VMEM((rows, LANES), F32),
                        pltpu.VMEM((rows, MLA_KV_LORA), F32)],
        compiler_params=_params("parallel", "parallel"),
        name="mla_prompt_attn",
    )(q_cat, k_cat)


def _mla_sample_kernel(pt_ref, q_ref, kn_ref, cc_ref, cp_ref, o_ref, m_ref, l_ref, acc_ref, cbuf, pbuf, sem,
                       *, layer, t_new, pps, n_steps, n_seq):
    b, s = pl.program_id(0), pl.program_id(1)
    slot = _page_ring(pt_ref, lambda ss: ss * pps, (cc_ref, cp_ref), (cbuf, pbuf), sem,
                      layer=layer, pps=pps, n_steps=n_steps, n_seq=n_seq)
    rows = MLA_HEADS * t_new
    q = q_ref[...]

    @pl.when(s == 0)
    def _():
        nb = kn_ref.shape[0]
        kn = kn_ref[...]
        sc = lax.dot_general(q, kn, _NT, preferred_element_type=F32)
        off = (b % (nb // t_new)) * t_new
        rel = lax.broadcasted_iota(jnp.int32, (rows, nb), 1) - off
        tok = _div(lax.broadcasted_iota(jnp.int32, (rows, nb), 0), MLA_HEADS)
        sc = jnp.where((rel >= 0) & (rel <= tok), sc, NEG_INF)
        m = jnp.max(sc, axis=1, keepdims=True)
        e = jnp.exp2(sc - m)
        m_ref[...] = jnp.broadcast_to(m, m_ref.shape)
        l_ref[...] = jnp.broadcast_to(jnp.sum(e, axis=1, keepdims=True), l_ref.shape)
        acc_ref[...] = jnp.dot(e.astype(BF16), kn[:, 0:MLA_KV_LORA], preferred_element_type=F32)

    ckv = jnp.concatenate([cbuf[slot, r] for r in range(pps)], axis=0).astype(BF16)
    kpe_t = jnp.concatenate([pbuf[slot, r] for r in range(pps)], axis=1).astype(BF16)
    sc = (lax.dot_general(q[:, 0:MLA_KV_LORA], ckv, _NT, preferred_element_type=F32)
          + jnp.dot(q[:, MLA_KV_LORA:MLA_KV_LORA + MLA_ROPE], kpe_t, preferred_element_type=F32))
    _softmax_step(sc, ckv, m_ref, l_ref, acc_ref)

    @pl.when(s == n_steps - 1)
    def _():
        o_ref[...] = (acc_ref[...] / _tile_lanes(l_ref[...], MLA_KV_LORA)).astype(o_ref.dtype)


def _mla_sample(q_s, k_cat, n_prompt, cache_ckv, cache_kpe_t, layer, page_table, t_new):
    db, n_pages = page_table.shape
    pps = math.gcd(MLA_PAGES_PER_STEP, n_pages)
    steps = n_pages // pps
    rows = MLA_HEADS * t_new
    nb = 16
    per = nb // t_new
    new_blk = n_prompt // nb

    kern = functools.partial(_mla_sample_kernel, layer=layer, t_new=t_new, pps=pps, n_steps=steps, n_seq=db)
    grid_spec = pltpu.PrefetchScalarGridSpec(
        num_scalar_prefetch=1,
        grid=(db, steps),
        in_specs=[
            pl.BlockSpec((rows, MLA_QW), lambda b, s, pt: (b, 0)),
            pl.BlockSpec((nb, MLA_QW), lambda b, s, pt: (new_blk + b // per, 0)),
            pl.BlockSpec(memory_space=pl.ANY),
            pl.BlockSpec(memory_space=pl.ANY),
        ],
        out_specs=pl.BlockSpec((rows, MLA_KV_LORA), lambda b, s, pt: (b, 0)),
        scratch_shapes=[pltpu.VMEM((rows, LANES), F32), pltpu.VMEM((rows, LANES), F32),
                        pltpu.VMEM((rows, MLA_KV_LORA), F32),
                        pltpu.VMEM((2, pps, PAGE_SIZE, MLA_KV_LORA), F32),
                        pltpu.VMEM((2, pps, MLA_ROPE, PAGE_SIZE), F32),
                        pltpu.SemaphoreType.DMA((2, 2))],
    )
    return pl.pallas_call(
        kern,
        grid_spec=grid_spec,
        out_shape=jax.ShapeDtypeStruct((db * rows, MLA_KV_LORA), BF16),
        compiler_params=_params("arbitrary", "arbitrary"),
        name="mla_sample_attn",
    )(page_table, q_s, k_cat, cache_ckv, cache_kpe_t)


def _mla_out_kernel(op_ref, os_ref, gate_ref, wuv_ref, w_ref, x_ref, g_ref, y_ref, *, prompt_blocks):
    tiles = op_ref.shape[0]
    is_prompt = pl.program_id(0) < prompt_blocks

    def head_rows(h):
        return jnp.concatenate([jnp.where(is_prompt, op_ref[t, h], os_ref[t, h]) for t in range(tiles)], axis=0)

    o = jnp.concatenate([jnp.dot(head_rows(h), wuv_ref[h], preferred_element_type=F32)
                         for h in range(MLA_HEADS)], axis=1)
    a = (o * _silu(gate_ref[...])).astype(BF16)
    y = jnp.dot(a, w_ref[...], preferred_element_type=F32)
    y_ref[...] = x_ref[...] + _rms(y, g_ref[...])


def _mla_out(o_p, o_s, p, wuv, w, x, g):
    n, d = x.shape
    tm = 256
    tt = tm // MLA_TQ
    npb = o_p.shape[0] // tt
    blk = (tt, MLA_HEADS, MLA_TQ, MLA_KV_LORA)
    kern = functools.partial(_mla_out_kernel, prompt_blocks=npb)
    return pl.pallas_call(
        kern,
        grid=(n // tm,),
        in_specs=[
            pl.BlockSpec(blk, lambda i: (jnp.minimum(i, npb - 1), 0, 0, 0)),
            pl.BlockSpec(blk, lambda i: (jnp.maximum(i - npb, 0), 0, 0, 0)),
            pl.BlockSpec((tm, d), lambda i: (i, 0)),
            pl.BlockSpec(wuv.shape, lambda i: (0, 0, 0)),
            pl.BlockSpec((d, d), lambda i: (0, 0)),
            pl.BlockSpec((tm, d), lambda i: (i, 0)),
            pl.BlockSpec((1, d), lambda i: (0, 0)),
        ],
        out_specs=pl.BlockSpec((tm, d), lambda i: (i, 0)),
        out_shape=jax.ShapeDtypeStruct((n, d), F32),
        compiler_params=_params("parallel"),
        name="mla_out_norm",
    )(o_p, o_s, p, wuv, w, x, g.reshape(1, d))


def _rope_tables(pos):
    half = ROPE_DIM // 2
    inv = jnp.float32(ROPE_THETA) ** (-jnp.arange(half, dtype=F32) * 2.0 / ROPE_DIM)
    ang = pos.astype(F32)[:, None] * inv[None, :]
    cos, sin = jnp.cos(ang), jnp.sin(ang)
    cos = jnp.concatenate([cos, cos, cos, cos], axis=1)
    sin = jnp.concatenate([-sin, sin, -sin, sin], axis=1)
    return cos, sin


def _pad_rows(a, n):
    return a if a.shape[0] == n else jnp.concatenate([a, jnp.zeros((n - a.shape[0],) + a.shape[1:], a.dtype)], axis=0)


def kernel(x_prompt, x_sample, cache_sb_k, cache_sb_v, state_swa_k, state_swa_v, cache_mla_ckv, cache_mla_kpe,
           page_table, norm_pre, norm_post, sb_w_in, sb_w_out, swa_w_in, swa_sinks, swa_w_out,
           mla_w_in, mla_q_norm, mla_w_uq, mla_kv_norm, mla_w_uk, mla_w_uv, mla_w_out):
    batch, seq, d = x_prompt.shape
    db, t_new, _ = x_sample.shape
    n_p, n_s = batch * seq, db * t_new
    n = n_p + n_s
    n_pad = -(-n // ROW_TILE) * ROW_TILE
    past = page_table.shape[1] * PAGE_SIZE
    x = _pad_rows(jnp.concatenate([x_prompt.reshape(n_p, d), x_sample.reshape(n_s, d)], axis=0), n_pad)
    pos = jnp.concatenate([jnp.tile(jnp.arange(seq), batch), jnp.tile(past + jnp.arange(t_new), db),
                           jnp.zeros((n_pad - n,), jnp.int32)])
    cos, sin = _rope_tables(pos)

    sb_st, swa_st, mla_st = [], [], []
    for i in range(DEPTH):
        kind, j = i % N_MIXERS, i // N_MIXERS
        if kind == 0:
            qo, ko, vo, go = 0, SB_WIDTH, SB_WIDTH + SB_KV_WIDTH, SB_WIDTH + 2 * SB_KV_WIDTH
            w_in = sb_w_in[j]
            w = jnp.concatenate([w_in[:, go:], w_in[:, qo:ko], w_in[:, ko:vo], w_in[:, vo:go]], axis=1).astype(BF16)
            p = _proj(x, norm_pre[i], w, cos, sin, tn=512)
            o_p = _sb_prompt(p, batch, seq)
            q_s = p[n_p:n, D_MODEL:D_MODEL + SB_WIDTH].reshape(db, t_new, SB_KV_HEADS, SB_GROUPS, SB_HEAD_DIM)
            q_s = q_s.transpose(0, 2, 3, 1, 4).reshape(db, SB_KV_HEADS, SB_GROUPS * t_new, SB_HEAD_DIM)
            o_s = _sb_sample(q_s, p, n_p, cache_sb_k, cache_sb_v, j, page_table)
            o_s = o_s.reshape(db, SB_KV_HEADS, SB_GROUPS, t_new, SB_HEAD_DIM).transpose(0, 3, 1, 2, 4).reshape(n_s, SB_WIDTH)
            x = _out(o_p, _pad_rows(o_s, n_pad - n_p), p, sb_w_out[j].astype(BF16), x, norm_post[i])
            k_all = p[:, D_MODEL + SB_WIDTH:D_MODEL + SB_WIDTH + SB_KV_WIDTH]
            v_all = p[:, D_MODEL + SB_WIDTH + SB_KV_WIDTH:]
            sb_st.append((k_all[:n_p].reshape(batch, seq, SB_KV_HEADS, SB_HEAD_DIM),
                          v_all[:n_p].reshape(batch, seq, SB_KV_HEADS, SB_HEAD_DIM),
                          k_all[n_p:n].reshape(db, t_new, SB_KV_HEADS, SB_HEAD_DIM),
                          v_all[n_p:n].reshape(db, t_new, SB_KV_HEADS, SB_HEAD_DIM)))
        elif kind == 1:
            qo, ko, vo, go = 0, SWA_WIDTH, SWA_WIDTH + SWA_KV_WIDTH, SWA_WIDTH + 2 * SWA_KV_WIDTH
            w_in = swa_w_in[j]
            w = jnp.concatenate([w_in[:, go:], w_in[:, qo:ko], w_in[:, ko:vo], w_in[:, vo:go]], axis=1).astype(BF16)
            p = _proj(x, norm_pre[i], w, cos, sin, tn=512, rope_cols=(D_MODEL, D_MODEL + SWA_WIDTH + SWA_KV_WIDTH))
            sinks = swa_sinks[j].astype(F32)
            o_p = _swa_prompt(p, sinks, batch, seq)
            w_buf = state_swa_k.shape[2]
            st_k = state_swa_k[j].reshape(db, w_buf, SWA_KV_WIDTH)
            st_v = state_swa_v[j].reshape(db, w_buf, SWA_KV_WIDTH)
            o_s = _swa_sample(p, sinks, n_p, st_k, st_v, t_new)
            x = _out(o_p, _pad_rows(o_s, n_pad - n_p), p, swa_w_out[j].astype(BF16), x, norm_post[i])
            k_all = p[:, D_MODEL + SWA_WIDTH:D_MODEL + SWA_WIDTH + SWA_KV_WIDTH]
            v_all = p[:, D_MODEL + SWA_WIDTH + SWA_KV_WIDTH:]
            keep = min(WINDOW, seq)
            kp = k_all[:n_p].reshape(batch, seq, SWA_KV_HEADS, SWA_HEAD_DIM)[:, seq - keep:]
            vp = v_all[:n_p].reshape(batch, seq, SWA_KV_HEADS, SWA_HEAD_DIM)[:, seq - keep:]
            ks = jnp.concatenate([st_k, k_all[n_p:n].reshape(db, t_new, SWA_KV_WIDTH)], axis=1)[:, t_new:]
            vs = jnp.concatenate([st_v, v_all[n_p:n].reshape(db, t_new, SWA_KV_WIDTH)], axis=1)[:, t_new:]
            swa_st.append((kp, vp, ks.reshape(db, w_buf, SWA_KV_HEADS, SWA_HEAD_DIM),
                           vs.reshape(db, w_buf, SWA_KV_HEADS, SWA_HEAD_DIM)))
        else:
            w_in = mla_w_in[j]
            o_kv = MLA_Q_LORA + MLA_KV_LORA + MLA_ROPE
            pad = jnp.zeros((d, 1024 - o_kv), w_in.dtype)
            w = jnp.concatenate([w_in[:, o_kv:], w_in[:, :o_kv], pad], axis=1).astype(BF16)
            p = _proj(x, norm_pre[i], w, cos, sin, tn=512)
            wuq = mla_w_uq[j].reshape(MLA_Q_LORA, MLA_HEADS, MLA_QK)
            wuq = jnp.concatenate([wuq[:, :, :MLA_NOPE].reshape(MLA_Q_LORA, -1),
                                   wuq[:, :, MLA_NOPE:].reshape(MLA_Q_LORA, -1)], axis=1).astype(BF16)
            wuk = mla_w_uk[j].transpose(1, 2, 0).astype(BF16)
            wuv = mla_w_uv[j].transpose(1, 0, 2).astype(BF16)
            q_cat, ckv, kpe, k_cat = _mla_prep(p, mla_q_norm[j], wuq, mla_kv_norm[j], wuk, cos, sin)
            o_p = _mla_prompt(q_cat, k_cat, batch, seq)
            tiles_p = n_p // MLA_TQ
            q_s = q_cat[tiles_p:].transpose(0, 2, 1, 3).reshape(-1, MLA_QW)
            o_s = _mla_sample(q_s, k_cat, n_p, cache_mla_ckv, cache_mla_kpe.transpose(0, 1, 3, 2), j, page_table, t_new)
            o_s = _pad_rows(o_s, (n_pad - n_p) * MLA_HEADS).reshape(-1, MLA_TQ, MLA_HEADS, MLA_KV_LORA).transpose(0, 2, 1, 3)
            x = _mla_out(o_p, o_s, p, wuv, mla_w_out[j].astype(BF16), x, norm_post[i])
            mla_st.append((ckv[:n_p].reshape(batch, seq, MLA_KV_LORA), kpe[:n_p].reshape(batch, seq, MLA_ROPE),
                           ckv[n_p:n].reshape(db, t_new, MLA_KV_LORA), kpe[n_p:n].reshape(db, t_new, MLA_ROPE)))

    def stack(st, k):
        return jnp.stack([s[k] for s in st])

    return (x[:n_p].reshape(batch, seq, d), x[n_p:n].reshape(db, t_new, d),
            stack(sb_st, 0), stack(sb_st, 1), stack(sb_st, 2), stack(sb_st, 3),
            stack(swa_st, 0), stack(swa_st, 1), stack(swa_st, 2), stack(swa_st, 3),
            stack(mla_st, 0), stack(mla_st, 1), stack(mla_st, 2), stack(mla_st, 3))
```

```python
import functools
import math

import jax
import jax.numpy as jnp
from jax import lax
from jax.experimental import pallas as pl
from jax.experimental.pallas import tpu as pltpu

F32 = jnp.float32
BF16 = jnp.bfloat16

D_MODEL = 2048
DEPTH = 4
N_MIXERS = 3
PAGE_SIZE = 128
ROPE_THETA = 10000.0
NORM_EPS = 1e-6
NEG_INF = -1e30
LOG2E = math.log2(math.e)

SB_HEADS = 16
SB_KV_HEADS = 4
SB_HEAD_DIM = D_MODEL // SB_HEADS
SB_GROUPS = SB_HEADS // SB_KV_HEADS
SB_WIDTH = SB_HEADS * SB_HEAD_DIM
SB_KV_WIDTH = SB_KV_HEADS * SB_HEAD_DIM

SWA_HEADS = 32
SWA_KV_HEADS = 8
SWA_HEAD_DIM = D_MODEL // SWA_HEADS
SWA_GROUPS = SWA_HEADS // SWA_KV_HEADS
SWA_WIDTH = SWA_HEADS * SWA_HEAD_DIM
SWA_KV_WIDTH = SWA_KV_HEADS * SWA_HEAD_DIM
WINDOW = 128

MLA_HEADS = 16
MLA_Q_LORA = D_MODEL // 4
MLA_KV_LORA = D_MODEL // 8
MLA_NOPE = 128
MLA_ROPE = 64
MLA_QK = MLA_NOPE + MLA_ROPE
MLA_V = 128
MLA_WIDTH = MLA_HEADS * MLA_V

LANES = 128
ROPE_DIM = 64
VMEM_LIMIT = 56 * 1024 * 1024

ROW_TILE = 512
PAGES_PER_STEP = 16
MLA_PAGES_PER_STEP = 32
MLA_TQ = 64
MLA_QW = MLA_KV_LORA + LANES

_NT = (((1,), (1,)), ((), ()))


def _params(*sem):
    return pltpu.CompilerParams(dimension_semantics=sem, vmem_limit_bytes=VMEM_LIMIT)


def _div(x, n):
    assert n & (n - 1) == 0
    return x >> (n.bit_length() - 1)


def _mod(x, n):
    assert n & (n - 1) == 0
    return x & (n - 1)


def _tile_lanes(x, width):
    reps = width // LANES
    return x if reps == 1 else jnp.concatenate([x] * reps, axis=1)


def _silu(x):
    return x / (1.0 + jnp.exp(-x))


def _rms(x, g):
    return x * lax.rsqrt(jnp.mean(x * x, axis=-1, keepdims=True) + NORM_EPS) * g


def _rope_lanes(x, cos, sin_signed):
    w = x.shape[1]
    cos, sin_signed = _tile_lanes(cos, w), _tile_lanes(sin_signed, w)
    lane = lax.broadcasted_iota(jnp.int32, x.shape, 1)
    first_half = (lane & (ROPE_DIM - 1)) < (ROPE_DIM // 2)
    partner = jnp.where(first_half, pltpu.roll(x, w - ROPE_DIM // 2, 1), pltpu.roll(x, ROPE_DIM // 2, 1))
    return x * cos + partner * sin_signed


def _proj_kernel(x_ref, g_ref, w_ref, cos_ref, sin_ref, o_ref, h_ref, *, tn, rope_lo, rope_hi):
    j = pl.program_id(1)

    @pl.when(j == 0)
    def _():
        h_ref[...] = _rms(x_ref[...], g_ref[...]).astype(BF16)

    acc = jnp.dot(h_ref[...], w_ref[...], preferred_element_type=F32)
    if rope_hi > rope_lo:
        first, last, part = rope_lo // tn, rope_hi // tn, rope_hi % tn
        in_rope = (j >= first) & (j < last)
        plain = jnp.logical_not(in_rope)

        @pl.when(in_rope)
        def _():
            o_ref[...] = _rope_lanes(acc, cos_ref[...], sin_ref[...])

        if part:
            plain = plain & (j != last)

            @pl.when(j == last)
            def _():
                o_ref[:, 0:part] = _rope_lanes(acc[:, 0:part], cos_ref[...], sin_ref[...])
                o_ref[:, part:] = acc[:, part:]

        @pl.when(plain)
        def _():
            o_ref[...] = acc
    else:
        o_ref[...] = acc


def _proj(x, g, w, cos, sin, *, tn, rope_cols=(0, 0)):
    n, d = x.shape
    c = w.shape[1]
    tm = ROW_TILE
    assert rope_cols[0] % tn == 0 and rope_cols[1] % LANES == 0
    kern = functools.partial(_proj_kernel, tn=tn, rope_lo=rope_cols[0], rope_hi=rope_cols[1])
    return pl.pallas_call(
        kern,
        grid=(n // tm, c // tn),
        in_specs=[
            pl.BlockSpec((tm, d), lambda i, j: (i, 0)),
            pl.BlockSpec((1, d), lambda i, j: (0, 0)),
            pl.BlockSpec((d, tn), lambda i, j: (0, j)),
            pl.BlockSpec((tm, LANES), lambda i, j: (i, 0)),
            pl.BlockSpec((tm, LANES), lambda i, j: (i, 0)),
        ],
        out_specs=pl.BlockSpec((tm, tn), lambda i, j: (i, j)),
        out_shape=jax.ShapeDtypeStruct((n, c), F32),
        scratch_shapes=[pltpu.VMEM((tm, d), BF16)],
        compiler_params=_params("parallel", "arbitrary"),
        name="norm_proj",
    )(x, g.reshape(1, d), w, cos, sin)


def _out_kernel(op_ref, os_ref, gate_ref, w_ref, x_ref, g_ref, y_ref, *, prompt_blocks):
    o = jnp.where(pl.program_id(0) < prompt_blocks, op_ref[...], os_ref[...])
    a = (o * _silu(gate_ref[...])).astype(BF16)
    y = jnp.dot(a, w_ref[...], preferred_element_type=F32)
    y_ref[...] = x_ref[...] + _rms(y, g_ref[...])


def _out(o_p, o_s, p, w, x, g):
    n, d = x.shape
    tm = 256
    npb = o_p.shape[0] // tm
    kern = functools.partial(_out_kernel, prompt_blocks=npb)
    return pl.pallas_call(
        kern,
        grid=(n // tm,),
        in_specs=[
            pl.BlockSpec((tm, d), lambda i: (jnp.minimum(i, npb - 1), 0)),
            pl.BlockSpec((tm, d), lambda i: (jnp.maximum(i - npb, 0), 0)),
            pl.BlockSpec((tm, d), lambda i: (i, 0)),
            pl.BlockSpec((d, d), lambda i: (0, 0)),
            pl.BlockSpec((tm, d), lambda i: (i, 0)),
            pl.BlockSpec((1, d), lambda i: (0, 0)),
        ],
        out_specs=pl.BlockSpec((tm, d), lambda i: (i, 0)),
        out_shape=jax.ShapeDtypeStruct((n, d), F32),
        compiler_params=_params("parallel"),
        name="gate_out_norm",
    )(o_p, o_s, p, w, x, g.reshape(1, d))


def _suffix_ones():
    s = lax.broadcasted_iota(jnp.int32, (LANES, LANES), 0)
    j = lax.broadcasted_iota(jnp.int32, (LANES, LANES), 1)
    u = (s > j).astype(BF16)
    return jnp.concatenate([u, u], axis=0)


def _sb_weights(t, carry, u2, valid):
    rows, width = t.shape
    nb = width // LANES
    drop = jnp.maximum(t, 0.0) + jnp.log(1.0 + jnp.exp2(-jnp.abs(t))) * LOG2E
    if valid is not None:
        drop = jnp.where(valid, drop, 0.0)
    hi = drop.astype(BF16)
    lo = (drop - hi.astype(F32)).astype(BF16)
    blocks = [jnp.concatenate([hi[:, p * LANES:(p + 1) * LANES], lo[:, p * LANES:(p + 1) * LANES]], axis=1)
              for p in range(nb)]
    later = jnp.dot(jnp.concatenate(blocks, axis=0), u2, preferred_element_type=F32)
    cols = [None] * nb
    for p in reversed(range(nb)):
        cols[p] = later[p * rows:(p + 1) * rows] + carry
        carry = carry + jnp.sum(drop[:, p * LANES:(p + 1) * LANES], axis=1, keepdims=True)
    w = jnp.exp2(t - drop - jnp.concatenate(cols, axis=1))
    if valid is not None:
        w = jnp.where(valid, w, 0.0)
    return w, carry


def _sb_prompt_kernel(q_ref, k_ref, v_ref, u_ref, o_ref, acc_ref, carry_ref, *, tq, tk):
    i = pl.program_id(2)
    groups = SB_GROUPS
    rows = groups * tq
    c = (SB_HEAD_DIM ** -0.5) * LOG2E
    q = jnp.concatenate([q_ref[:, g * SB_HEAD_DIM:(g + 1) * SB_HEAD_DIM] for g in range(groups)], axis=0)
    q = (q * c).astype(BF16)
    u2 = u_ref[...]
    diag = (i * tq) // tk

    def block(kb, valid):
        start = pl.multiple_of(kb * tk, tk)
        k = k_ref[pl.ds(start, tk), :].astype(BF16)
        v = v_ref[pl.ds(start, tk), :].astype(BF16)
        t = lax.dot_general(q, k, _NT, preferred_element_type=F32)
        w, carry = _sb_weights(t, carry_ref[...], u2, valid)
        carry_ref[...] = carry
        acc_ref[...] += jnp.dot(w.astype(BF16), v, preferred_element_type=F32)

    acc_ref[...] = jnp.zeros_like(acc_ref)
    carry_ref[...] = jnp.zeros_like(carry_ref)
    q_pos = i * tq + _mod(lax.broadcasted_iota(jnp.int32, (rows, tk), 0), tq)
    k_pos = diag * tk + lax.broadcasted_iota(jnp.int32, (rows, tk), 1)
    block(diag, k_pos < q_pos)

    def body(s, carry):
        block(diag - 1 - s, None)
        return carry

    lax.fori_loop(0, diag, body, 0)
    for g in range(groups):
        o_ref[:, g * SB_HEAD_DIM:(g + 1) * SB_HEAD_DIM] = acc_ref[g * tq:(g + 1) * tq, :]


def _sb_prompt(p, batch, seq):
    tq, tk = 512, 512
    nq = seq // tq
    qw = SB_GROUPS * SB_HEAD_DIM
    q_col, k_col, v_col = D_MODEL // qw, (D_MODEL + SB_WIDTH) // SB_HEAD_DIM, (D_MODEL + SB_WIDTH + SB_KV_WIDTH) // SB_HEAD_DIM
    kern = functools.partial(_sb_prompt_kernel, tq=tq, tk=tk)
    return pl.pallas_call(
        kern,
        grid=(batch, SB_KV_HEADS, nq),
        in_specs=[
            pl.BlockSpec((tq, qw), lambda b, h, i: (b * nq + i, q_col + h)),
            pl.BlockSpec((seq, SB_HEAD_DIM), lambda b, h, i: (b, k_col + h)),
            pl.BlockSpec((seq, SB_HEAD_DIM), lambda b, h, i: (b, v_col + h)),
            pl.BlockSpec((2 * LANES, LANES), lambda b, h, i: (0, 0)),
        ],
        out_specs=pl.BlockSpec((tq, qw), lambda b, h, i: (b * nq + i, h)),
        out_shape=jax.ShapeDtypeStruct((batch * seq, SB_WIDTH), F32),
        scratch_shapes=[pltpu.VMEM((SB_GROUPS * tq, SB_HEAD_DIM), F32), pltpu.VMEM((SB_GROUPS * tq, LANES), F32)],
        compiler_params=_params("parallel", "parallel", "parallel"),
        name="sb_prompt_attn",
    )(p, p, p, _suffix_ones())


def _page_ring(pt_ref, page_of_step, srcs, bufs, sem, *, layer, pps, n_steps, n_seq):
    b, s = pl.program_id(0), pl.program_id(1)
    g = b * n_steps + s
    slot = lax.rem(g, 2)

    def copies(bb, ss, sl):
        out = []
        for r in range(pps):
            page = pt_ref[bb, page_of_step(ss) + r]
            for k, (src, buf) in enumerate(zip(srcs, bufs)):
                out.append(pltpu.make_async_copy(src.at[layer, page], buf.at[sl, r], sem.at[sl, k]))
        return out

    @pl.when(g == 0)
    def _():
        for c in copies(0, 0, 0):
            c.start()

    @pl.when(g + 1 < n_seq * n_steps)
    def _():
        wrap = s == n_steps - 1
        for c in copies(jnp.where(wrap, b + 1, b), jnp.where(wrap, 0, s + 1), 1 - slot):
            c.start()

    for c in copies(b, s, slot):
        c.wait()
    return slot


def _sb_sample_kernel(pt_ref, q_ref, kn_ref, vn_ref, ck_ref, cv_ref, u_ref, o_ref,
                      acc_ref, carry_ref, kpad_ref, vpad_ref, kbuf, vbuf, sem, *, layer, t_new, pps, n_steps, n_seq):
    b, s = pl.program_id(0), pl.program_id(1)
    slot = _page_ring(pt_ref, lambda ss: (n_steps - 1 - ss) * pps, (ck_ref, cv_ref), (kbuf, vbuf), sem,
                      layer=layer, pps=pps, n_steps=n_steps, n_seq=n_seq)
    kvh, hd = SB_KV_HEADS, SB_HEAD_DIM
    rq = SB_GROUPS * t_new
    c = (hd ** -0.5) * LOG2E
    u2 = u_ref[...]
    qs = [(q_ref[h] * c).astype(BF16) for h in range(kvh)]

    def attend(k_heads, v_heads, valid):
        t = jnp.concatenate([lax.dot_general(qs[h], k_heads[h], _NT, preferred_element_type=F32)
                             for h in range(kvh)], axis=0)
        w, carry = _sb_weights(t, carry_ref[...], u2, valid)
        carry_ref[...] = carry
        w = w.astype(BF16)
        for h in range(kvh):
            acc_ref[h] += jnp.dot(w[h * rq:(h + 1) * rq], v_heads[h], preferred_element_type=F32)

    @pl.when(s == 0)
    def _():
        acc_ref[...] = jnp.zeros_like(acc_ref)
        carry_ref[...] = jnp.zeros_like(carry_ref)
        kpad_ref[...] = jnp.zeros_like(kpad_ref)
        vpad_ref[...] = jnp.zeros_like(vpad_ref)
        nb = kn_ref.shape[0]
        kpad_ref[0:nb, :] = kn_ref[...]
        vpad_ref[0:nb, :] = vn_ref[...]
        off = (b % (nb // t_new)) * t_new
        rel = lax.broadcasted_iota(jnp.int32, (kvh * rq, PAGE_SIZE), 1) - off
        tok = _mod(lax.broadcasted_iota(jnp.int32, (kvh * rq, PAGE_SIZE), 0), t_new)
        attend([kpad_ref[:, h * hd:(h + 1) * hd].astype(BF16) for h in range(kvh)],
               [vpad_ref[:, h * hd:(h + 1) * hd].astype(BF16) for h in range(kvh)],
               (rel >= 0) & (rel < tok))

    def head_rows(buf, h):
        return jnp.concatenate([buf[slot, r, pl.ds(h, PAGE_SIZE, stride=kvh), :] for r in range(pps)],
                               axis=0).astype(BF16)

    attend([head_rows(kbuf, h) for h in range(kvh)], [head_rows(vbuf, h) for h in range(kvh)], None)

    @pl.when(s == n_steps - 1)
    def _():
        o_ref[...] = acc_ref[...]


def _sb_sample(q_s, p, n_prompt, cache_k, cache_v, layer, page_table):
    db, kvh, rq, hd = q_s.shape
    t_new = rq // SB_GROUPS
    n_pages = page_table.shape[1]
    pps = math.gcd(PAGES_PER_STEP, n_pages)
    steps = n_pages // pps
    nb = 8
    kw = SB_KV_WIDTH
    k_col, v_col = (D_MODEL + SB_WIDTH) // kw, (D_MODEL + SB_WIDTH + kw) // kw
    new_blk = n_prompt // nb
    per = nb // t_new
    ck = cache_k.reshape(cache_k.shape[0], cache_k.shape[1], PAGE_SIZE * kvh, hd)
    cv = cache_v.reshape(cache_v.shape[0], cache_v.shape[1], PAGE_SIZE * kvh, hd)

    kern = functools.partial(_sb_sample_kernel, layer=layer, t_new=t_new, pps=pps, n_steps=steps, n_seq=db)
    grid_spec = pltpu.PrefetchScalarGridSpec(
        num_scalar_prefetch=1,
        grid=(db, steps),
        in_specs=[
            pl.BlockSpec((None, kvh, rq, hd), lambda b, s, pt: (b, 0, 0, 0)),
            pl.BlockSpec((nb, kw), lambda b, s, pt: (new_blk + b // per, k_col)),
            pl.BlockSpec((nb, kw), lambda b, s, pt: (new_blk + b // per, v_col)),
            pl.BlockSpec(memory_space=pl.ANY),
            pl.BlockSpec(memory_space=pl.ANY),
            pl.BlockSpec((2 * LANES, LANES), lambda b, s, pt: (0, 0)),
        ],
        out_specs=pl.BlockSpec((None, kvh, rq, hd), lambda b, s, pt: (b, 0, 0, 0)),
        scratch_shapes=[
            pltpu.VMEM((kvh, rq, hd), F32),
            pltpu.VMEM((kvh * rq, LANES), F32),
            pltpu.VMEM((PAGE_SIZE, kw), F32),
            pltpu.VMEM((PAGE_SIZE, kw), F32),
            pltpu.VMEM((2, pps, PAGE_SIZE * kvh, hd), F32),
            pltpu.VMEM((2, pps, PAGE_SIZE * kvh, hd), F32),
            pltpu.SemaphoreType.DMA((2, 2)),
        ],
    )
    return pl.pallas_call(
        kern,
        grid_spec=grid_spec,
        out_shape=jax.ShapeDtypeStruct((db, kvh, rq, hd), F32),
        compiler_params=_params("arbitrary", "arbitrary"),
        name="sb_sample_attn",
    )(page_table, q_s, p, p, ck, cv, _suffix_ones())


def _lane_halves(x):
    lane = lax.broadcasted_iota(jnp.int32, x.shape, 1)
    lo = lane < ROPE_DIM
    z = jnp.zeros_like(x)
    a0 = jnp.where(lo, x, z)
    b1 = jnp.where(lo, z, x)
    return a0, pltpu.roll(a0, ROPE_DIM, 1), pltpu.roll(b1, ROPE_DIM, 1), b1


def _swa_heads(q, k, v, valid, sinks_ref, sink_base):
    ka0, ka1, kb0, kb1 = _lane_halves(k.astype(BF16))
    va0, va1, vb0, vb1 = _lane_halves(v.astype(BF16))
    outs = []
    for m in range(4):
        qm = q[:, m * LANES:(m + 1) * LANES].astype(BF16)
        k0, k1, v0, v1 = (ka0, ka1, va0, va1) if m < 2 else (kb0, kb1, vb0, vb1)
        o = None
        for half, (kk, vv) in enumerate(((k0, v0), (k1, v1))):
            s = lax.dot_general(qm, kk, _NT, preferred_element_type=F32) * (SWA_HEAD_DIM ** -0.5)
            s = jnp.where(valid, s, NEG_INF)
            sink = sinks_ref[sink_base + 2 * m + half]
            mx = jnp.maximum(jnp.max(s, axis=-1, keepdims=True), sink)
            e = jnp.exp(s - mx)
            inv = 1.0 / (jnp.sum(e, axis=-1, keepdims=True) + jnp.exp(sink - mx))
            part = jnp.dot((e * inv).astype(BF16), vv, preferred_element_type=F32)
            o = part if o is None else o + part
        outs.append(o)
    return jnp.concatenate(outs, axis=1)


def _swa_prompt_kernel(sinks_ref, q_ref, kp_ref, kc_ref, vp_ref, vc_ref, o_ref):
    i, c = pl.program_id(1), pl.program_id(2)
    tq, tp = q_ref.shape[0], kp_ref.shape[0]
    k = jnp.concatenate([kp_ref[...], kc_ref[...]], axis=0)
    v = jnp.concatenate([vp_ref[...], vc_ref[...]], axis=0)
    qi = lax.broadcasted_iota(jnp.int32, (tq, tp + tq), 0)
    kj = lax.broadcasted_iota(jnp.int32, (tq, tp + tq), 1) - tp
    diff = qi - kj
    valid = (diff >= 0) & (diff < WINDOW) & (kj + i * tq >= 0)
    o_ref[...] = _swa_heads(q_ref[...], k, v, valid, sinks_ref, 8 * c)


def _swa_prompt(p, sinks, batch, seq):
    tp = WINDOW
    per = 2 if seq % (2 * WINDOW) == 0 else 1
    tq = per * WINDOW
    nq = seq // tq
    qw = 8 * SWA_HEAD_DIM
    q_col = D_MODEL // qw
    k_col = (D_MODEL + SWA_WIDTH) // LANES
    v_col = (D_MODEL + SWA_WIDTH + SWA_KV_WIDTH) // LANES

    def prev(b, i):
        return b * nq * per + jnp.maximum(i * per - 1, 0)

    grid_spec = pltpu.PrefetchScalarGridSpec(
        num_scalar_prefetch=1,
        grid=(batch, nq, SWA_KV_HEADS // 2),
        in_specs=[
            pl.BlockSpec((tq, qw), lambda b, i, c, sk: (b * nq + i, q_col + c)),
            pl.BlockSpec((tp, LANES), lambda b, i, c, sk: (prev(b, i), k_col + c)),
            pl.BlockSpec((tq, LANES), lambda b, i, c, sk: (b * nq + i, k_col + c)),
            pl.BlockSpec((tp, LANES), lambda b, i, c, sk: (prev(b, i), v_col + c)),
            pl.BlockSpec((tq, LANES), lambda b, i, c, sk: (b * nq + i, v_col + c)),
        ],
        out_specs=pl.BlockSpec((tq, qw), lambda b, i, c, sk: (b * nq + i, c)),
    )
    return pl.pallas_call(
        _swa_prompt_kernel,
        grid_spec=grid_spec,
        out_shape=jax.ShapeDtypeStruct((batch * seq, SWA_WIDTH), F32),
        compiler_params=_params("parallel", "parallel", "parallel"),
        name="swa_prompt_attn",
    )(sinks, p, p, p, p, p)


def _swa_sample_kernel(sinks_ref, q_ref, kn_ref, vn_ref, ks_ref, vs_ref, o_ref, *, t_new):
    c = pl.program_id(1)
    nb = q_ref.shape[0]
    per = nb // t_new
    w_buf = ks_ref.shape[1]
    k = jnp.concatenate([ks_ref[s] for s in range(per)] + [kn_ref[...]], axis=0)
    v = jnp.concatenate([vs_ref[s] for s in range(per)] + [vn_ref[...]], axis=0)
    nk = per * w_buf + nb
    row = lax.broadcasted_iota(jnp.int32, (nb, nk), 0)
    col = lax.broadcasted_iota(jnp.int32, (nb, nk), 1)
    q_seq, q_tok = _div(row, t_new), _mod(row, t_new)
    in_state = col < per * w_buf
    new_col = jnp.maximum(col - per * w_buf, 0)
    k_seq = jnp.where(in_state, _div(col, w_buf), _div(new_col, t_new))
    k_idx = jnp.where(in_state, _mod(col, w_buf), w_buf + _mod(new_col, t_new))
    diff = w_buf + q_tok - k_idx
    valid = (k_seq == q_seq) & (diff >= 0) & (diff < WINDOW)
    o_ref[...] = _swa_heads(q_ref[...], k, v, valid, sinks_ref, 8 * c)


def _swa_sample(p, sinks, n_prompt, state_k, state_v, t_new):
    db, w_buf, kw = state_k.shape
    nb = 32 if (db * t_new) % 32 == 0 else 8
    per = nb // t_new
    qw = 8 * SWA_HEAD_DIM
    q_col = D_MODEL // qw
    k_col = (D_MODEL + SWA_WIDTH) // LANES
    v_col = (D_MODEL + SWA_WIDTH + SWA_KV_WIDTH) // LANES
    new_blk = n_prompt // nb
    kern = functools.partial(_swa_sample_kernel, t_new=t_new)
    grid_spec = pltpu.PrefetchScalarGridSpec(
        num_scalar_prefetch=1,
        grid=(db // per, SWA_KV_HEADS // 2),
        in_specs=[
            pl.BlockSpec((nb, qw), lambda g, c, sk: (new_blk + g, q_col + c)),
            pl.BlockSpec((nb, LANES), lambda g, c, sk: (new_blk + g, k_col + c)),
            pl.BlockSpec((nb, LANES), lambda g, c, sk: (new_blk + g, v_col + c)),
            pl.BlockSpec((per, w_buf, LANES), lambda g, c, sk: (g, 0, c)),
            pl.BlockSpec((per, w_buf, LANES), lambda g, c, sk: (g, 0, c)),
        ],
        out_specs=pl.BlockSpec((nb, qw), lambda g, c, sk: (g, c)),
    )
    return pl.pallas_call(
        kern,
        grid_spec=grid_spec,
        out_shape=jax.ShapeDtypeStruct((db * t_new, SWA_WIDTH), F32),
        compiler_params=_params("parallel", "parallel"),
        name="swa_sample_attn",
    )(sinks, p, p, p, state_k, state_v)


def _mla_prep_kernel(p_ref, qn_ref, wuq_ref, kvn_ref, wuk_ref, cos_ref, sin_ref,
                     qcat_ref, ckv_ref, kpe_ref, kcat_ref):
    tm = p_ref.shape[0]
    c = (MLA_QK ** -0.5) * LOG2E
    c_q = _rms(p_ref[:, 0:MLA_Q_LORA], qn_ref[...]).astype(BF16)
    c_kv = _rms(p_ref[:, MLA_Q_LORA:MLA_Q_LORA + MLA_KV_LORA], kvn_ref[...])
    cos, sin = cos_ref[...], sin_ref[...]
    o = MLA_Q_LORA + MLA_KV_LORA
    k_pe = _rope_lanes(p_ref[:, o:o + LANES], cos, sin)
    ckv_ref[...] = c_kv
    kpe_ref[...] = k_pe[:, 0:MLA_ROPE]
    kcat_ref[:, 0:MLA_KV_LORA] = c_kv.astype(BF16)
    kcat_ref[:, MLA_KV_LORA:] = k_pe.astype(BF16)
    q = jnp.dot(c_q, wuq_ref[...], preferred_element_type=F32)
    q_pe = _rope_lanes(q[:, MLA_HEADS * MLA_NOPE:], cos, sin) * c
    lane = lax.broadcasted_iota(jnp.int32, (tm, LANES), 1)
    for h in range(MLA_HEADS):
        qn = q[:, h * MLA_NOPE:(h + 1) * MLA_NOPE].astype(BF16)
        q_lat = (jnp.dot(qn, wuk_ref[h], preferred_element_type=F32) * c).astype(BF16)
        x = q_pe[:, (h // 2) * LANES:(h // 2 + 1) * LANES]
        if h % 2:
            x = pltpu.roll(x, MLA_ROPE, 1)
        pe = jnp.where(lane < MLA_ROPE, x, 0.0).astype(BF16)
        for tile in range(tm // MLA_TQ):
            r = slice(tile * MLA_TQ, (tile + 1) * MLA_TQ)
            qcat_ref[tile, h, :, 0:MLA_KV_LORA] = q_lat[r]
            qcat_ref[tile, h, :, MLA_KV_LORA:] = pe[r]


def _mla_prep(p, q_norm, wuq, kv_norm, wuk, cos, sin):
    n = p.shape[0]
    tm = 256
    pw = p.shape[1] - D_MODEL
    return pl.pallas_call(
        _mla_prep_kernel,
        grid=(n // tm,),
        in_specs=[
            pl.BlockSpec((tm, pw), lambda i: (i, D_MODEL // pw)),
            pl.BlockSpec((1, MLA_Q_LORA), lambda i: (0, 0)),
            pl.BlockSpec(wuq.shape, lambda i: (0, 0)),
            pl.BlockSpec((1, MLA_KV_LORA), lambda i: (0, 0)),
            pl.BlockSpec(wuk.shape, lambda i: (0, 0, 0)),
            pl.BlockSpec((tm, LANES), lambda i: (i, 0)),
            pl.BlockSpec((tm, LANES), lambda i: (i, 0)),
        ],
        out_specs=[
            pl.BlockSpec((tm // MLA_TQ, MLA_HEADS, MLA_TQ, MLA_QW), lambda i: (i, 0, 0, 0)),
            pl.BlockSpec((tm, MLA_KV_LORA), lambda i: (i, 0)),
            pl.BlockSpec((tm, MLA_ROPE), lambda i: (i, 0)),
            pl.BlockSpec((tm, MLA_QW), lambda i: (i, 0)),
        ],
        out_shape=[
            jax.ShapeDtypeStruct((n // MLA_TQ, MLA_HEADS, MLA_TQ, MLA_QW), BF16),
            jax.ShapeDtypeStruct((n, MLA_KV_LORA), F32),
            jax.ShapeDtypeStruct((n, MLA_ROPE), F32),
            jax.ShapeDtypeStruct((n, MLA_QW), BF16),
        ],
        compiler_params=_params("parallel"),
        name="mla_prep",
    )(p, q_norm.reshape(1, -1), wuq, kv_norm.reshape(1, -1), wuk, cos, sin)


def _softmax_step(s, v, m_ref, l_ref, acc_ref):
    m_old = m_ref[...]
    m_new = jnp.maximum(m_old, jnp.max(s, axis=1, keepdims=True))
    alpha = jnp.exp2(m_old - m_new)
    e = jnp.exp2(s - _tile_lanes(m_new, s.shape[1]))
    l_ref[...] = alpha * l_ref[...] + jnp.sum(e, axis=1, keepdims=True)
    acc_ref[...] = _tile_lanes(alpha, acc_ref.shape[1]) * acc_ref[...] + jnp.dot(
        e.astype(BF16), v, preferred_element_type=F32)
    m_ref[...] = m_new


def _mla_prompt_kernel(q_ref, k_ref, o_ref, m_ref, l_ref, acc_ref, *, tk):
    i = pl.program_id(1)
    tq = MLA_TQ
    rows = MLA_HEADS * tq
    q = jnp.concatenate([q_ref[h] for h in range(MLA_HEADS)], axis=0)
    m_ref[...] = jnp.full_like(m_ref, NEG_INF)
    l_ref[...] = jnp.zeros_like(l_ref)
    acc_ref[...] = jnp.zeros_like(acc_ref)
    diag = (i * tq) // tk

    def block(kb, masked):
        start = pl.multiple_of(kb * tk, tk)
        kc = k_ref[pl.ds(start, tk), :]
        s = lax.dot_general(q, kc, _NT, preferred_element_type=F32)
        if masked:
            q_pos = i * tq + _mod(lax.broadcasted_iota(jnp.int32, (rows, tk), 0), tq)
            k_pos = kb * tk + lax.broadcasted_iota(jnp.int32, (rows, tk), 1)
            s = jnp.where(k_pos <= q_pos, s, NEG_INF)
        _softmax_step(s, kc[:, 0:MLA_KV_LORA], m_ref, l_ref, acc_ref)

    def body(kb, carry):
        block(kb, False)
        return carry

    lax.fori_loop(0, diag, body, 0)
    block(diag, True)
    o = (acc_ref[...] / _tile_lanes(l_ref[...], MLA_KV_LORA)).astype(o_ref.dtype)
    for h in range(MLA_HEADS):
        o_ref[h] = o[h * tq:(h + 1) * tq]


def _mla_prompt(q_cat, k_cat, batch, seq):
    tk = 512
    nq = seq // MLA_TQ
    rows = MLA_HEADS * MLA_TQ
    kern = functools.partial(_mla_prompt_kernel, tk=tk)
    return pl.pallas_call(
        kern,
        grid=(batch, nq),
        in_specs=[
            pl.BlockSpec((None, MLA_HEADS, MLA_TQ, MLA_QW), lambda b, i: (b * nq + i, 0, 0, 0)),
            pl.BlockSpec((seq, MLA_QW), lambda b, i: (b, 0)),
        ],
        out_specs=pl.BlockSpec((None, MLA_HEADS, MLA_TQ, MLA_KV_LORA), lambda b, i: (b * nq + i, 0, 0, 0)),
        out_shape=jax.ShapeDtypeStruct((batch * nq, MLA_HEADS, MLA_TQ, MLA_KV_LORA), BF16),
        scratch_shapes=[pltpu.VMEM((rows, LANES), F32), pltpu.VMEM((rows, LANES), F32),
                        pltpu.VMEM((rows, MLA_KV_LORA), F32)],
        compiler_params=_params("parallel", "parallel"),
        name="mla_prompt_attn",
    )(q_cat, k_cat)


def _mla_sample_kernel(pt_ref, q_ref, kn_ref, cc_ref, cp_ref, o_ref, m_ref, l_ref, acc_ref, cbuf, pbuf, sem,
                       *, layer, t_new, pps, n_steps, n_seq):
    b, s = pl.program_id(0), pl.program_id(1)
    slot = _page_ring(pt_ref, lambda ss: ss * pps, (cc_ref, cp_ref), (cbuf, pbuf), sem,
                      layer=layer, pps=pps, n_steps=n_steps, n_seq=n_seq)
    rows = MLA_HEADS * t_new
    q = q_ref[...]

    @pl.when(s == 0)
    def _():
        nb = kn_ref.shape[0]
        kn = kn_ref[...]
        sc = lax.dot_general(q, kn, _NT, preferred_element_type=F32)
        off = (b % (nb // t_new)) * t_new
        rel = lax.broadcasted_iota(jnp.int32, (rows, nb), 1) - off
        tok = _div(lax.broadcasted_iota(jnp.int32, (rows, nb), 0), MLA_HEADS)
        sc = jnp.where((rel >= 0) & (rel <= tok), sc, NEG_INF)
        m = jnp.max(sc, axis=1, keepdims=True)
        e = jnp.exp2(sc - m)
        m_ref[...] = jnp.broadcast_to(m, m_ref.shape)
        l_ref[...] = jnp.broadcast_to(jnp.sum(e, axis=1, keepdims=True), l_ref.shape)
        acc_ref[...] = jnp.dot(e.astype(BF16), kn[:, 0:MLA_KV_LORA], preferred_element_type=F32)

    ckv = jnp.concatenate([cbuf[slot, r] for r in range(pps)], axis=0).astype(BF16)
    kpe_t = jnp.concatenate([pbuf[slot, r] for r in range(pps)], axis=1).astype(BF16)
    sc = (lax.dot_general(q[:, 0:MLA_KV_LORA], ckv, _NT, preferred_element_type=F32)
          + jnp.dot(q[:, MLA_KV_LORA:MLA_KV_LORA + MLA_ROPE], kpe_t, preferred_element_type=F32))
    _softmax_step(sc, ckv, m_ref, l_ref, acc_ref)

    @pl.when(s == n_steps - 1)
    def _():
        o_ref[...] = (acc_ref[...] / _tile_lanes(l_ref[...], MLA_KV_LORA)).astype(o_ref.dtype)


def _mla_sample(q_s, k_cat, n_prompt, cache_ckv, cache_kpe_t, layer, page_table, t_new):
    db, n_pages = page_table.shape
    pps = math.gcd(MLA_PAGES_PER_STEP, n_pages)
    steps = n_pages // pps
    rows = MLA_HEADS * t_new
    nb = 16
    per = nb // t_new
    new_blk = n_prompt // nb

    kern = functools.partial(_mla_sample_kernel, layer=layer, t_new=t_new, pps=pps, n_steps=steps, n_seq=db)
    grid_spec = pltpu.PrefetchScalarGridSpec(
        num_scalar_prefetch=1,
        grid=(db, steps),
        in_specs=[
            pl.BlockSpec((rows, MLA_QW), lambda b, s, pt: (b, 0)),
            pl.BlockSpec((nb, MLA_QW), lambda b, s, pt: (new_blk + b // per, 0)),
            pl.BlockSpec(memory_space=pl.ANY),
            pl.BlockSpec(memory_space=pl.ANY),
        ],
        out_specs=pl.BlockSpec((rows, MLA_KV_LORA), lambda b, s, pt: (b, 0)),
        scratch_shapes=[pltpu.VMEM((rows, LANES), F32), pltpu.VMEM((rows, LANES), F32),
                        pltpu.VMEM((rows, MLA_KV_LORA), F32),
                        pltpu.VMEM((2, pps, PAGE_SIZE, MLA_KV_LORA), F32),
                        pltpu.VMEM((2, pps, MLA_ROPE, PAGE_SIZE), F32),
                        pltpu.SemaphoreType.DMA((2, 2))],
    )
    return pl.pallas_call(
        kern,
        grid_spec=grid_spec,
        out_shape=jax.ShapeDtypeStruct((db * rows, MLA_KV_LORA), BF16),
        compiler_params=_params("arbitrary", "arbitrary"),
        name="mla_sample_attn",
    )(page_table, q_s, k_cat, cache_ckv, cache_kpe_t)


def _mla_out_kernel(op_ref, os_ref, gate_ref, wuv_ref, w_ref, x_ref, g_ref, y_ref, *, prompt_blocks):
    tiles = op_ref.shape[0]
    is_prompt = pl.program_id(0) < prompt_blocks

    def head_rows(h):
        return jnp.concatenate([jnp.where(is_prompt, op_ref[t, h], os_ref[t, h]) for t in range(tiles)], axis=0)

    o = jnp.concatenate([jnp.dot(head_rows(h), wuv_ref[h], preferred_element_type=F32)
                         for h in range(MLA_HEADS)], axis=1)
    a = (o * _silu(gate_ref[...])).astype(BF16)
    y = jnp.dot(a, w_ref[...], preferred_element_type=F32)
    y_ref[...] = x_ref[...] + _rms(y, g_ref[...])


def _mla_out(o_p, o_s, p, wuv, w, x, g):
    n, d = x.shape
    tm = 256
    tt = tm // MLA_TQ
    npb = o_p.shape[0] // tt
    blk = (tt, MLA_HEADS, MLA_TQ, MLA_KV_LORA)
    kern = functools.partial(_mla_out_kernel, prompt_blocks=npb)
    return pl.pallas_call(
        kern,
        grid=(n // tm,),
        in_specs=[
            pl.BlockSpec(blk, lambda i: (jnp.minimum(i, npb - 1), 0, 0, 0)),
            pl.BlockSpec(blk, lambda i: (jnp.maximum(i - npb, 0), 0, 0, 0)),
            pl.BlockSpec((tm, d), lambda i: (i, 0)),
            pl.BlockSpec(wuv.shape, lambda i: (0, 0, 0)),
            pl.BlockSpec((d, d), lambda i: (0, 0)),
            pl.BlockSpec((tm, d), lambda i: (i, 0)),
            pl.BlockSpec((1, d), lambda i: (0, 0)),
        ],
        out_specs=pl.BlockSpec((tm, d), lambda i: (i, 0)),
        out_shape=jax.ShapeDtypeStruct((n, d), F32),
        compiler_params=_params("parallel"),
        name="mla_out_norm",
    )(o_p, o_s, p, wuv, w, x, g.reshape(1, d))


def _rope_tables(pos):
    half = ROPE_DIM // 2
    inv = jnp.float32(ROPE_THETA) ** (-jnp.arange(half, dtype=F32) * 2.0 / ROPE_DIM)
    ang = pos.astype(F32)[:, None] * inv[None, :]
    cos, sin = jnp.cos(ang), jnp.sin(ang)
    cos = jnp.concatenate([cos, cos, cos, cos], axis=1)
    sin = jnp.concatenate([-sin, sin, -sin, sin], axis=1)
    return cos, sin


def _pad_rows(a, n):
    return a if a.shape[0] == n else jnp.concatenate([a, jnp.zeros((n - a.shape[0],) + a.shape[1:], a.dtype)], axis=0)


def kernel(x_prompt, x_sample, cache_sb_k, cache_sb_v, state_swa_k, state_swa_v, cache_mla_ckv, cache_mla_kpe,
           page_table, norm_pre, norm_post, sb_w_in, sb_w_out, swa_w_in, swa_sinks, swa_w_out,
           mla_w_in, mla_q_norm, mla_w_uq, mla_kv_norm, mla_w_uk, mla_w_uv, mla_w_out):
    batch, seq, d = x_prompt.shape
    db, t_new, _ = x_sample.shape
    n_p, n_s = batch * seq, db * t_new
    n = n_p + n_s
    n_pad = -(-n // ROW_TILE) * ROW_TILE
    past = page_table.shape[1] * PAGE_SIZE
    x = _pad_rows(jnp.concatenate([x_prompt.reshape(n_p, d), x_sample.reshape(n_s, d)], axis=0), n_pad)
    pos = jnp.concatenate([jnp.tile(jnp.arange(seq), batch), jnp.tile(past + jnp.arange(t_new), db),
                           jnp.zeros((n_pad - n,), jnp.int32)])
    cos, sin = _rope_tables(pos)

    sb_st, swa_st, mla_st = [], [], []
    for i in range(DEPTH):
        kind, j = i % N_MIXERS, i // N_MIXERS
        if kind == 0:
            qo, ko, vo, go = 0, SB_WIDTH, SB_WIDTH + SB_KV_WIDTH, SB_WIDTH + 2 * SB_KV_WIDTH
            w_in = sb_w_in[j]
            w = jnp.concatenate([w_in[:, go:], w_in[:, qo:ko], w_in[:, ko:vo], w_in[:, vo:go]], axis=1).astype(BF16)
            p = _proj(x, norm_pre[i], w, cos, sin, tn=1024)
            o_p = _sb_prompt(p, batch, seq)
            q_s = p[n_p:n, D_MODEL:D_MODEL + SB_WIDTH].reshape(db, t_new, SB_KV_HEADS, SB_GROUPS, SB_HEAD_DIM)
            q_s = q_s.transpose(0, 2, 3, 1, 4).reshape(db, SB_KV_HEADS, SB_GROUPS * t_new, SB_HEAD_DIM)
            o_s = _sb_sample(q_s, p, n_p, cache_sb_k, cache_sb_v, j, page_table)
            o_s = o_s.reshape(db, SB_KV_HEADS, SB_GROUPS, t_new, SB_HEAD_DIM).transpose(0, 3, 1, 2, 4).reshape(n_s, SB_WIDTH)
            x = _out(o_p, _pad_rows(o_s, n_pad - n_p), p, sb_w_out[j].astype(BF16), x, norm_post[i])
            k_all = p[:, D_MODEL + SB_WIDTH:D_MODEL + SB_WIDTH + SB_KV_WIDTH]
            v_all = p[:, D_MODEL + SB_WIDTH + SB_KV_WIDTH:]
            sb_st.append((k_all[:n_p].reshape(batch, seq, SB_KV_HEADS, SB_HEAD_DIM),
                          v_all[:n_p].reshape(batch, seq, SB_KV_HEADS, SB_HEAD_DIM),
                          k_all[n_p:n].reshape(db, t_new, SB_KV_HEADS, SB_HEAD_DIM),
                          v_all[n_p:n].reshape(db, t_new, SB_KV_HEADS, SB_HEAD_DIM)))
        elif kind == 1:
            qo, ko, vo, go = 0, SWA_WIDTH, SWA_WIDTH + SWA_KV_WIDTH, SWA_WIDTH + 2 * SWA_KV_WIDTH
            w_in = swa_w_in[j]
            w = jnp.concatenate([w_in[:, go:], w_in[:, qo:ko], w_in[:, ko:vo], w_in[:, vo:go]], axis=1).astype(BF16)
            p = _proj(x, norm_pre[i], w, cos, sin, tn=1024, rope_cols=(D_MODEL, D_MODEL + SWA_WIDTH + SWA_KV_WIDTH))
            sinks = swa_sinks[j].astype(F32)
            o_p = _swa_prompt(p, sinks, batch, seq)
            w_buf = state_swa_k.shape[2]
            st_k = state_swa_k[j].reshape(db, w_buf, SWA_KV_WIDTH)
            st_v = state_swa_v[j].reshape(db, w_buf, SWA_KV_WIDTH)
            o_s = _swa_sample(p, sinks, n_p, st_k, st_v, t_new)
            x = _out(o_p, _pad_rows(o_s, n_pad - n_p), p, swa_w_out[j].astype(BF16), x, norm_post[i])
            k_all = p[:, D_MODEL + SWA_WIDTH:D_MODEL + SWA_WIDTH + SWA_KV_WIDTH]
            v_all = p[:, D_MODEL + SWA_WIDTH + SWA_KV_WIDTH:]
            keep = min(WINDOW, seq)
            kp = k_all[:n_p].reshape(batch, seq, SWA_KV_HEADS, SWA_HEAD_DIM)[:, seq - keep:]
            vp = v_all[:n_p].reshape(batch, seq, SWA_KV_HEADS, SWA_HEAD_DIM)[:, seq - keep:]
            ks = jnp.concatenate([st_k, k_all[n_p:n].reshape(db, t_new, SWA_KV_WIDTH)], axis=1)[:, t_new:]
            vs = jnp.concatenate([st_v, v_all[n_p:n].reshape(db, t_new, SWA_KV_WIDTH)], axis=1)[:, t_new:]
            swa_st.append((kp, vp, ks.reshape(db, w_buf, SWA_KV_HEADS, SWA_HEAD_DIM),
                           vs.reshape(db, w_buf, SWA_KV_HEADS, SWA_HEAD_DIM)))
        else:
            w_in = mla_w_in[j]
            o_kv = MLA_Q_LORA + MLA_KV_LORA + MLA_ROPE
            pad = jnp.zeros((d, 1024 - o_kv), w_in.dtype)
            w = jnp.concatenate([w_in[:, o_kv:], w_in[:, :o_kv], pad], axis=1).astype(BF16)
            p = _proj(x, norm_pre[i], w, cos, sin, tn=1024)
            wuq = mla_w_uq[j].reshape(MLA_Q_LORA, MLA_HEADS, MLA_QK)
            wuq = jnp.concatenate([wuq[:, :, :MLA_NOPE].reshape(MLA_Q_LORA, -1),
                                   wuq[:, :, MLA_NOPE:].reshape(MLA_Q_LORA, -1)], axis=1).astype(BF16)
            wuk = mla_w_uk[j].transpose(1, 2, 0).astype(BF16)
            wuv = mla_w_uv[j].transpose(1, 0, 2).astype(BF16)
            q_cat, ckv, kpe, k_cat = _mla_prep(p, mla_q_norm[j], wuq, mla_kv_norm[j], wuk, cos, sin)
            o_p = _mla_prompt(q_cat, k_cat, batch, seq)
            tiles_p = n_p // MLA_TQ
            q_s = q_cat[tiles_p:].transpose(0, 2, 1, 3).reshape(-1, MLA_QW)
            o_s = _mla_sample(q_s, k_cat, n_p, cache_mla_ckv, cache_mla_kpe.transpose(0, 1, 3, 2), j, page_table, t_new)
            o_s = _pad_rows(o_s, (n_pad - n_p) * MLA_HEADS).reshape(-1, MLA_TQ, MLA_HEADS, MLA_KV_LORA).transpose(0, 2, 1, 3)
            x = _mla_out(o_p, o_s, p, wuv, mla_w_out[j].astype(BF16), x, norm_post[i])
            mla_st.append((ckv[:n_p].reshape(batch, seq, MLA_KV_LORA), kpe[:n_p].reshape(batch, seq, MLA_ROPE),
                           ckv[n_p:n].reshape(db, t_new, MLA_KV_LORA), kpe[n_p:n].reshape(db, t_new, MLA_ROPE)))

    def stack(st, k):
        return jnp.stack([s[k] for s in st])

    return (x[:n_p].reshape(batch, seq, d), x[n_p:n].reshape(db, t_new, d),
            stack(sb_st, 0), stack(sb_st, 1), stack(sb_st, 2), stack(sb_st, 3),
            stack(swa_st, 0), stack(swa_st, 1), stack(swa_st, 2), stack(swa_st, 3),
            stack(mla_st, 0), stack(mla_st, 1), stack(mla_st, 2), stack(mla_st, 3))
```
